```python
import math
import jax, jax.numpy as jnp
from jax import lax
import numpy as np

D_MODEL = 2048
BATCH = 4
SEQ = 2048
DEPTH = 4

GRID_W = 64
CTX_LEN = 256
N_BRANCH = 3
ATT_HEADS = D_MODEL // 256
ATT_HEAD_DIM = 64
ATT_V_DIM = 2 * ATT_HEAD_DIM
ATT_WIDTH = ATT_HEADS * 2 * ATT_HEAD_DIM
ROPE_BASE = 10000.0
Q_BLOCK = 128
HY_WIDTH = D_MODEL // 4
HY_ORDER = 2
HY_BANDS = 16
HY_EMB_DIM = 1 + 2 * HY_BANDS
HY_FILTER_DIM = 64
HY_MIN_DECAY = math.log(1e-2) / 1.5
HY_MAX_DECAY = math.log(1e-2) / 0.3
POOL_WINDOWS = (2, 4, 8, 16)
POOL_GROUPS = len(POOL_WINDOWS)
POOL_WIDTH = D_MODEL // 4
POOL_GROUP_DIM = POOL_WIDTH // POOL_GROUPS
IN_SPLITS = (ATT_WIDTH, 2 * ATT_WIDTH, 3 * ATT_WIDTH, 3 * ATT_WIDTH + 3 * HY_WIDTH,
             3 * ATT_WIDTH + 3 * HY_WIDTH + POOL_WIDTH)
IN_WIDTH = 3 * ATT_WIDTH + 3 * HY_WIDTH + POOL_WIDTH + N_BRANCH * D_MODEL
FF_DIM = 5632
EPS = 1e-6

kernel_name = "hybrid_diffattn_hyena_pool_dit"

F32 = jnp.float32


def rmsnorm(x, g):
    xf = x.astype(F32)
    y = xf * lax.rsqrt(jnp.mean(xf * xf, axis=-1, keepdims=True) + EPS)
    return (y * g.astype(F32)).astype(x.dtype)


def dwconv3(x, w):
    xp = jnp.pad(x, ((0, 0), (1, 1), (0, 0)))
    return xp[:, :-2] * w[0] + xp[:, 1:-1] * w[1] + xp[:, 2:] * w[2]


def axial_rope(rows):
    row = jnp.repeat(jnp.arange(rows, dtype=F32), GRID_W)
    col = jnp.tile(jnp.arange(GRID_W, dtype=F32), rows)
    n_freq = ATT_HEAD_DIM // 4
    inv = ROPE_BASE ** (-jnp.arange(n_freq, dtype=F32) / n_freq)
    ang = jnp.stack([row[:, None] * inv, col[:, None] * inv], axis=1)
    return jnp.cos(ang), jnp.sin(ang)


def apply_rope(t, cos, sin):
    B, L, H, M, _ = t.shape
    tr = t.astype(F32).reshape(B, L, H, M, 2, 2, ATT_HEAD_DIM // 4)
    c = cos[None, :, None, None]
    s = sin[None, :, None, None]
    t1, t2 = tr[..., 0, :], tr[..., 1, :]
    out = jnp.stack([t1 * c - t2 * s, t2 * c + t1 * s], axis=-2)
    return out.reshape(t.shape).astype(t.dtype)


def heads_qk(t):
    B, L, _ = t.shape
    return t.reshape(B, L, ATT_HEADS, 2, ATT_HEAD_DIM)


def heads_v(t):
    B, L, _ = t.shape
    return t.reshape(B, L, ATT_HEADS, ATT_V_DIM)


def diff_lambda(lam_p, layer_idx):
    lam_init = 0.8 - 0.6 * math.exp(-0.3 * layer_idx)
    lp = lam_p.astype(F32)
    lam = jnp.exp(jnp.sum(lp[0] * lp[1])) - jnp.exp(jnp.sum(lp[2] * lp[3])) + lam_init
    return lam, lam_init


def diff_attend(q, k, v, lam):
    s = jnp.einsum('bqhmd,bkhmd->bhmqk', q, k).astype(F32) * (ATT_HEAD_DIM ** -0.5)
    p = jax.nn.softmax(s, axis=-1).astype(v.dtype)
    o = jnp.einsum('bhmqk,bkhe->bqhme', p, v)
    return o[:, :, :, 0] - lam.astype(o.dtype) * o[:, :, :, 1]


def blocked_diff_attention(q, k, v, lam):
    B, L = q.shape[0], q.shape[1]
    nb = L // Q_BLOCK
    qb = jnp.moveaxis(q.reshape(B, nb, Q_BLOCK, ATT_HEADS, 2, ATT_HEAD_DIM), 1, 0)
    ob = lax.map(lambda qq: diff_attend(qq, k, v, lam), qb)
    return jnp.moveaxis(ob, 0, 1).reshape(B, L, ATT_HEADS, ATT_V_DIM)


def hyena_filters(L, w1, b1, w2, b2, w3, freq):
    t = jnp.linspace(0.0, 1.0, L, dtype=F32)[:, None]
    w_ang = 2.0 * math.pi * jnp.arange(L, dtype=F32)[:, None] / L
    f = jnp.linspace(1e-4, HY_BANDS - 1, HY_BANDS, dtype=F32)[None, :]
    z = jnp.concatenate([t, jnp.cos(f * w_ang), -jnp.sin(f * w_ang)], axis=-1)
    fr = freq.astype(F32)
    h = jnp.sin(fr[0] * (z @ w1.astype(F32) + b1.astype(F32)))
    h = jnp.sin(fr[1] * (h @ w2.astype(F32) + b2.astype(F32)))
    h = (h @ w3.astype(F32)).reshape(L, 2, HY_ORDER, HY_WIDTH)
    deltas = jnp.abs(jnp.linspace(HY_MIN_DECAY, HY_MAX_DECAY, HY_WIDTH, dtype=F32))
    h = h * jnp.exp(-t * deltas[None, :])[:, None, None, :]
    fwd, bwd = h[:, 0], h[:, 1]
    k = jnp.concatenate([fwd, jnp.zeros((1, HY_ORDER, HY_WIDTH), F32), bwd[:0:-1]], axis=0)
    return k / (jnp.sum(jnp.abs(k), axis=0, keepdims=True) + EPS)


def fft_longconv(u, k, bias):
    L = u.shape[1]
    uf = jnp.fft.rfft(u.astype(F32), n=2 * L, axis=1)
    kf = jnp.fft.rfft(k, n=2 * L, axis=0)
    y = jnp.fft.irfft(uf * kf[None], n=2 * L, axis=1)[:, :L]
    return (y + u.astype(F32) * bias.astype(F32)).astype(u.dtype)


def hyena_mix(hy, short_w, filt, bias):
    hy = dwconv3(hy, short_w)
    v, x1, x2 = jnp.split(hy, 3, axis=-1)
    z = x1 * fft_longconv(v, filt[:, 0], bias[0])
    return x2 * fft_longconv(z, filt[:, 1], bias[1])


def pool_mix(p, pool_w, pool_scale):
    B, L, _ = p.shape
    pg = p.reshape(B, L, POOL_GROUPS, POOL_GROUP_DIM)
    t = jnp.arange(L)
    outs = []
    for g, win in enumerate(POOL_WINDOWS):
        xg = pg[:, :, g].astype(F32)
        cs = jnp.pad(jnp.cumsum(xg, axis=1), ((0, 0), (1, 0), (0, 0)))
        lo = jnp.clip(t - win // 2, 0, L)
        hi = jnp.clip(t + win - win // 2, 0, L)
        mean = (cs[:, hi] - cs[:, lo]) / (hi - lo).astype(F32)[None, :, None]
        outs.append((mean - xg).astype(p.dtype) @ pool_w[g])
    return jnp.concatenate(outs, axis=-1) * pool_scale


def mixer_branches(q, k_all, v_all, hy, pool, gates, lam, lam_init, filt, subln_g, hy_short_w, hy_bias,
                   pool_w, pool_scale, w_att_o, w_hy_o, w_pool_o, w_out):
    B, L = hy.shape[0], hy.shape[1]
    att = blocked_diff_attention(q, k_all, v_all, lam)
    att = (rmsnorm(att, subln_g) * (1.0 - lam_init)).reshape(B, L, ATT_WIDTH)
    hyo = hyena_mix(hy, hy_short_w, filt, hy_bias)
    poo = pool_mix(pool, pool_w, pool_scale)
    g_a, g_h, g_p = jnp.split(jax.nn.sigmoid(gates), N_BRANCH, axis=-1)
    merged = g_a * (att @ w_att_o) + g_h * (hyo @ w_hy_o) + g_p * (poo @ w_pool_o)
    return merged @ w_out


def conv_ffn(h, w_up, conv_w, w_down):
    gate, val = jnp.split(h @ w_up, 2, axis=-1)
    return (jax.nn.gelu(dwconv3(gate, conv_w)) * val) @ w_down


def setup_inputs(seed: int = 0) -> dict:
    key = jax.random.key(seed)
    ks = jax.random.split(key, 32)

    def nrm(k, shape, fan_in):
        return jax.random.normal(k, shape, F32) * (fan_in ** -0.5)

    def near_one(k, shape):
        return 1.0 + 0.02 * jax.random.normal(k, shape, F32)

    return {
        "x": jax.random.normal(ks[0], (BATCH, SEQ, D_MODEL), F32),
        "c": jax.random.normal(ks[1], (BATCH, D_MODEL), F32),
        "ctx": jax.random.normal(ks[2], (BATCH, CTX_LEN, D_MODEL), F32),
        "c_ctx": jax.random.normal(ks[3], (D_MODEL,), F32),
        "w_ada": nrm(ks[4], (DEPTH, D_MODEL, 6 * D_MODEL), D_MODEL),
        "b_ada": 0.02 * jax.random.normal(ks[5], (DEPTH, 6 * D_MODEL), F32),
        "norm_g": near_one(ks[6], (DEPTH, 4, D_MODEL)),
        "w_in": nrm(ks[7], (DEPTH, D_MODEL, IN_WIDTH), D_MODEL),
        "diff_lam": 0.1 * jax.random.normal(ks[8], (DEPTH, 4, ATT_HEAD_DIM), F32),
        "attn_subln_g": near_one(ks[9], (DEPTH, ATT_V_DIM)),
        "hy_short_w": nrm(ks[10], (DEPTH, 3, 3 * HY_WIDTH), 3),
        "hy_ffn_w1": nrm(ks[11], (DEPTH, HY_EMB_DIM, HY_FILTER_DIM), HY_EMB_DIM),
        "hy_ffn_b1": 0.1 * jax.random.normal(ks[12], (DEPTH, HY_FILTER_DIM), F32),
        "hy_ffn_w2": nrm(ks[13], (DEPTH, HY_FILTER_DIM, HY_FILTER_DIM), HY_FILTER_DIM),
        "hy_ffn_b2": 0.1 * jax.random.normal(ks[14], (DEPTH, HY_FILTER_DIM), F32),
        "hy_ffn_w3": nrm(ks[15], (DEPTH, HY_FILTER_DIM, 2 * HY_ORDER * HY_WIDTH), HY_FILTER_DIM),
        "hy_freq": near_one(ks[16], (DEPTH, 2, HY_FILTER_DIM)),
        "hy_bias": 0.5 * jax.random.normal(ks[17], (DEPTH, HY_ORDER, HY_WIDTH), F32),
        "pool_w": nrm(ks[18], (DEPTH, POOL_GROUPS, POOL_GROUP_DIM, POOL_GROUP_DIM), POOL_GROUP_DIM),
        "pool_scale": near_one(ks[19], (DEPTH, POOL_WIDTH)),
        "w_att_o": nrm(ks[20], (DEPTH, ATT_WIDTH, D_MODEL), ATT_WIDTH),
        "w_hy_o": nrm(ks[21], (DEPTH, HY_WIDTH, D_MODEL), HY_WIDTH),
        "w_pool_o": nrm(ks[22], (DEPTH, POOL_WIDTH, D_MODEL), POOL_WIDTH),
        "w_out": nrm(ks[23], (DEPTH, D_MODEL, D_MODEL), D_MODEL),
        "w_up": nrm(ks[24], (DEPTH, D_MODEL, 2 * FF_DIM), D_MODEL),
        "ff_conv_w": nrm(ks[25], (DEPTH, 3, FF_DIM), 3),
        "w_down": nrm(ks[26], (DEPTH, FF_DIM, D_MODEL), FF_DIM),
    }


def reference(x, c, ctx, c_ctx, w_ada, b_ada, norm_g, w_in, diff_lam, attn_subln_g, hy_short_w,
              hy_ffn_w1, hy_ffn_b1, hy_ffn_w2, hy_ffn_b2, hy_ffn_w3, hy_freq, hy_bias, pool_w, pool_scale,
              w_att_o, w_hy_o, w_pool_o, w_out, w_up, ff_conv_w, w_down):
    L = x.shape[1]
    LC = ctx.shape[1]
    ROWS = L // GRID_W
    cos, sin = axial_rope(ROWS)
    silu_c = jax.nn.silu(c)
    silu_cc = jax.nn.silu(c_ctx)
    for l in range(DEPTH):
        last = l == DEPTH - 1
        m_x = [m[:, None, :] for m in jnp.split(silu_c @ w_ada[l] + b_ada[l], 6, axis=-1)]
        m_c = jnp.split(silu_cc @ w_ada[l] + b_ada[l], 6, axis=-1)
        lam, lam_init = diff_lambda(diff_lam[l], l)
        filt_x = hyena_filters(L, hy_ffn_w1[l], hy_ffn_b1[l], hy_ffn_w2[l], hy_ffn_b2[l], hy_ffn_w3[l], hy_freq[l])

        hx = rmsnorm(x, norm_g[l, 0]) * (1.0 + m_x[1]) + m_x[0]
        hc = rmsnorm(ctx, norm_g[l, 0]) * (1.0 + m_c[1]) + m_c[0]
        if last:
            k_c, v_c = jnp.split(hc @ w_in[l][:, ATT_WIDTH:3 * ATT_WIDTH], 2, axis=-1)
        else:
            q_c, k_c, v_c, hy_c, pool_c, gate_c = jnp.split(hc @ w_in[l], IN_SPLITS, axis=-1)
        k_c = heads_qk(k_c)
        v_c = heads_v(v_c)

        q_x, k_x, v_x, hy_x, pool_x, gate_x = jnp.split(hx @ w_in[l], IN_SPLITS, axis=-1)
        q_x = apply_rope(heads_qk(q_x), cos, sin)
        k_x = apply_rope(heads_qk(k_x), cos, sin)
        k_all = jnp.concatenate([k_x, k_c], axis=1)
        v_all = jnp.concatenate([heads_v(v_x), v_c], axis=1)
        mix_x = mixer_branches(q_x, k_all, v_all, hy_x, pool_x, gate_x, lam, lam_init, filt_x,
                               attn_subln_g[l], hy_short_w[l], hy_bias[l], pool_w[l], pool_scale[l],
                               w_att_o[l], w_hy_o[l], w_pool_o[l], w_out[l])
        x = x + m_x[2] * rmsnorm(mix_x, norm_g[l, 1])

        h2 = rmsnorm(x, norm_g[l, 2]) * (1.0 + m_x[4]) + m_x[3]
        x = x + m_x[5] * rmsnorm(conv_ffn(h2, w_up[l], ff_conv_w[l], w_down[l]), norm_g[l, 3])

        if not last:
            filt_c = hyena_filters(LC, hy_ffn_w1[l], hy_ffn_b1[l], hy_ffn_w2[l], hy_ffn_b2[l], hy_ffn_w3[l], hy_freq[l])
            mix_c = mixer_branches(heads_qk(q_c), k_c, v_c, hy_c, pool_c, gate_c, lam, lam_init, filt_c,
                                   attn_subln_g[l], hy_short_w[l], hy_bias[l], pool_w[l], pool_scale[l],
                                   w_att_o[l], w_hy_o[l], w_pool_o[l], w_out[l])
            ctx = ctx + m_c[2] * rmsnorm(mix_c, norm_g[l, 1])
            hc2 = rmsnorm(ctx, norm_g[l, 2]) * (1.0 + m_c[4]) + m_c[3]
            ctx = ctx + m_c[5] * rmsnorm(conv_ffn(hc2, w_up[l], ff_conv_w[l], w_down[l]), norm_g[l, 3])
    return x
```

```python
import functools
import math

import jax
import jax.numpy as jnp
from jax import lax
from jax.experimental import pallas as pl
from jax.experimental.pallas import tpu as pltpu

F32 = jnp.float32
BF16 = jnp.bfloat16

GRID_W = 64
N_BRANCH = 3
ATT_HEAD_DIM = 64
ATT_V_DIM = 2 * ATT_HEAD_DIM
ROPE_BASE = 10000.0
HY_ORDER = 2
HY_BANDS = 16
HY_EMB_DIM = 1 + 2 * HY_BANDS
HY_MIN_DECAY = math.log(1e-2) / 1.5
HY_MAX_DECAY = math.log(1e-2) / 0.3
POOL_WINDOWS = (2, 4, 8, 16)
EPS = 1e-6

V7X_VMEM_BYTES = 64 * 1024 * 1024
VMEM_LIMIT = V7X_VMEM_BYTES - 8 * 1024 * 1024
LANE = 128


def _params(n_axes):
    return pltpu.CompilerParams(dimension_semantics=("arbitrary",) * n_axes,
                                vmem_limit_bytes=VMEM_LIMIT)


def _bdot(a, b):
    return jnp.dot(a.astype(BF16), b.astype(BF16), preferred_element_type=F32)


def _sigmoid(x):
    return 1.0 / (1.0 + jnp.exp(-x))


def _rms(x, g):
    return x * lax.rsqrt(jnp.mean(x * x, axis=-1, keepdims=True) + EPS) * g


def _ada_kernel(c_ref, w_ref, b_ref, o_ref):
    c = c_ref[...]
    o_ref[...] = _bdot(c * _sigmoid(c), w_ref[...]) + b_ref[...]


def _ada(cond, w_ada, b_ada):
    depth, d, n = w_ada.shape
    rows = cond.shape[0]
    tn = 1024
    return pl.pallas_call(
        _ada_kernel,
        grid=(depth, n // tn),
        in_specs=[pl.BlockSpec((rows, d), lambda l, j: (0, 0)),
                  pl.BlockSpec((None, d, tn), lambda l, j: (l, 0, j)),
                  pl.BlockSpec((None, 1, tn), lambda l, j: (l, 0, j))],
        out_specs=pl.BlockSpec((None, rows, tn), lambda l, j: (l, 0, j)),
        out_shape=jax.ShapeDtypeStruct((depth, rows, n), F32),
        compiler_params=_params(2),
        name="ada",
    )(cond, w_ada, b_ada.reshape(depth, 1, n))


def _resnorm_kernel(*refs, has_res, has_norm):
    it = iter(refs)
    x_ref = next(it)
    if has_res:
        y_ref, gres_ref, gate_ref = next(it), next(it), next(it)
    if has_norm:
        gnorm_ref, shift_ref, scale_ref = next(it), next(it), next(it)
    x = x_ref[...]
    if has_res:
        x = x + gate_ref[...] * _rms(y_ref[...], gres_ref[...])
        xo_ref = next(it)
        xo_ref[...] = x
    if has_norm:
        h_ref = next(it)
        h_ref[...] = (_rms(x, gnorm_ref[...]) * (1.0 + scale_ref[...]) + shift_ref[...]).astype(BF16)


def _resnorm(x, y=None, g_res=None, gate=None, g_norm=None, shift=None, scale=None):
    g, s, d = x.shape
    ts = 256
    has_res, has_norm = y is not None, g_norm is not None
    row = pl.BlockSpec((None, ts, d), lambda a, i: (a, i, 0))
    vec = pl.BlockSpec((1, d), lambda a, i: (0, 0))
    mod = pl.BlockSpec((None, 1, d), lambda a, i: (a, 0, 0))
    args, specs, outs, out_specs = [x], [row], [], []
    if has_res:
        args += [y, g_res.reshape(1, d), gate]
        specs += [row, vec, mod]
        outs.append(jax.ShapeDtypeStruct((g, s, d), F32))
        out_specs.append(row)
    if has_norm:
        args += [g_norm.reshape(1, d), shift, scale]
        specs += [vec, mod, mod]
        outs.append(jax.ShapeDtypeStruct((g, s, d), BF16))
        out_specs.append(row)
    res = pl.pallas_call(
        functools.partial(_resnorm_kernel, has_res=has_res, has_norm=has_norm),
        grid=(g, s // ts), in_specs=specs, out_specs=out_specs, out_shape=outs,
        compiler_params=_params(2), name="resnorm",
    )(*args)
    return res if len(res) > 1 else res[0]


def _mm_kernel(a_ref, w_ref, o_ref):
    o_ref[...] = _bdot(a_ref[...], w_ref[...]).astype(o_ref.dtype)


def _mm_rope_kernel(a_ref, w_ref, co_ref, sa_ref, sb_ref, o_ref, *, rope_tiles):
    acc = _bdot(a_ref[...], w_ref[...])
    j = pl.program_id(2)

    @pl.when(j < rope_tiles)
    def _():
        co, sa, sb = co_ref[...], sa_ref[...], sb_ref[...]
        quarter = ATT_HEAD_DIM // 4
        for c in range(acc.shape[1] // LANE):
            t = acc[:, c * LANE:(c + 1) * LANE]
            r = t * co + pltpu.roll(t, LANE - quarter, 1) * sa + pltpu.roll(t, quarter, 1) * sb
            o_ref[:, c * LANE:(c + 1) * LANE] = r.astype(o_ref.dtype)

    @pl.when(j >= rope_tiles)
    def _():
        o_ref[...] = acc.astype(o_ref.dtype)


def _mm(a, w, layer, col_off, n, out_dtype, *, tm, tn, rope=None, rope_cols=0, name="mm"):
    g, s, k = a.shape
    assert col_off % tn == 0 and n % tn == 0 and s % tm == 0
    off = col_off // tn
    in_specs = [pl.BlockSpec((None, tm, k), lambda a_, i, j: (a_, i, 0)),
                pl.BlockSpec((None, k, tn), lambda a_, i, j: (layer, 0, off + j))]
    args = [a, w]
    if rope is None:
        kern = _mm_kernel
    else:
        kern = functools.partial(_mm_rope_kernel, rope_tiles=rope_cols // tn)
        in_specs += [pl.BlockSpec((tm, LANE), lambda a_, i, j: (i, 0))] * 3
        args += list(rope)
    return pl.pallas_call(
        kern, grid=(g, s // tm, n // tn), in_specs=in_specs,
        out_specs=pl.BlockSpec((None, tm, tn), lambda a_, i, j: (a_, i, j)),
        out_shape=jax.ShapeDtypeStruct((g, s, n), out_dtype),
        compiler_params=_params(3), name=name,
    )(*args)


def _mm_acc_kernel(a_ref, w_ref, o_ref):
    part = _bdot(a_ref[...], w_ref[...])

    @pl.when(pl.program_id(1) == 0)
    def _():
        o_ref[...] = part

    @pl.when(pl.program_id(1) > 0)
    def _():
        o_ref[...] += part


def _mm_acc(a, w, layer, *, tm, tk):
    r, k = a.shape
    n = w.shape[2]
    return pl.pallas_call(
        _mm_acc_kernel, grid=(r // tm, k // tk),
        in_specs=[pl.BlockSpec((tm, tk), lambda i, kk: (i, kk)),
                  pl.BlockSpec((None, tk, n), lambda i, kk: (layer, kk, 0))],
        out_specs=pl.BlockSpec((tm, n), lambda i, kk: (i, 0)),
        out_shape=jax.ShapeDtypeStruct((r, n), F32),
        compiler_params=_params(2), name="mm_acc",
    )(a, w)


def _attn_kernel(*refs, has_ctx, lam_init, s_own, s_ctx):
    if has_ctx:
        q_ref, k_ref, v_ref, kc_ref, vc_ref, lam_ref, g_ref, o_ref, kk, vv = refs
    else:
        q_ref, k_ref, v_ref, lam_ref, g_ref, o_ref, kk, vv = refs

    @pl.when(pl.program_id(2) == 0)
    def _():
        kk[0:s_own, :] = k_ref[...]
        vv[0:s_own, :] = v_ref[...]
        if has_ctx:
            kk[s_own:s_own + s_ctx, :] = kc_ref[...]
            vv[s_own:s_own + s_ctx, :] = vc_ref[...]

    lp = lam_ref[...]
    lam = (jnp.exp(jnp.sum(lp[0:1] * lp[1:2], axis=-1, keepdims=True))
           - jnp.exp(jnp.sum(lp[2:3] * lp[3:4], axis=-1, keepdims=True)) + lam_init)

    q = q_ref[...] * (ATT_HEAD_DIM ** -0.5)
    tq = q.shape[0]
    lane = lax.broadcasted_iota(jnp.int32, q.shape, 1)
    zero = jnp.zeros_like(q)
    q2 = jnp.concatenate([jnp.where(lane < ATT_HEAD_DIM, q, zero),
                          jnp.where(lane >= ATT_HEAD_DIM, q, zero)], axis=0)
    s = lax.dot_general(q2, kk[...], (((1,), (1,)), ((), ())), preferred_element_type=F32)
    m = jnp.max(s, axis=-1, keepdims=True)
    e = jnp.exp(s - m)
    den = jnp.sum(e, axis=-1, keepdims=True)
    o = jnp.dot(e.astype(BF16), vv[...], preferred_element_type=F32) / den
    att = o[:tq] - lam * o[tq:]
    o_ref[...] = (_rms(att, g_ref[...]) * (1.0 - lam_init)).astype(o_ref.dtype)


def _attention(qkv, qkv_ctx, diff_lam, subln_g, layer, lam_init, *, tq):
    b, s, w3 = qkv.shape
    hd = ATT_V_DIM
    heads = w3 // (3 * hd)
    has_ctx = qkv_ctx is not None
    s_ctx = qkv_ctx.shape[1] if has_ctx else 0
    in_specs = [pl.BlockSpec((None, tq, hd), lambda b_, h, i: (b_, i, h)),
                pl.BlockSpec((None, s, hd), lambda b_, h, i: (b_, 0, heads + h)),
                pl.BlockSpec((None, s, hd), lambda b_, h, i: (b_, 0, 2 * heads + h))]
    args = [qkv, qkv, qkv]
    if has_ctx:
        in_specs += [pl.BlockSpec((None, s_ctx, hd), lambda b_, h, i: (b_, 0, heads + h)),
                     pl.BlockSpec((None, s_ctx, hd), lambda b_, h, i: (b_, 0, 2 * heads + h))]
        args += [qkv_ctx, qkv_ctx]
    in_specs += [pl.BlockSpec((None, 4, ATT_HEAD_DIM), lambda b_, h, i: (layer, 0, 0)),
                 pl.BlockSpec((None, 1, hd), lambda b_, h, i: (layer, 0, 0))]
    args += [diff_lam, subln_g.reshape(subln_g.shape[0], 1, hd)]
    return pl.pallas_call(
        functools.partial(_attn_kernel, has_ctx=has_ctx, lam_init=lam_init, s_own=s, s_ctx=s_ctx),
        grid=(b, heads, s // tq), in_specs=in_specs,
        out_specs=pl.BlockSpec((None, tq, hd), lambda b_, h, i: (b_, i, h)),
        out_shape=jax.ShapeDtypeStruct((b, s, heads * hd), BF16),
        scratch_shapes=[pltpu.VMEM((s + s_ctx, hd), BF16), pltpu.VMEM((s + s_ctx, hd), BF16)],
        compiler_params=_params(3), name="attn",
    )(*args)


def _shift_rows(x, d, seg):
    n = x.shape[0]
    if d == 0:
        return x
    rolled = pltpu.roll(x, (-d) % n, 0)
    pos = lax.broadcasted_iota(jnp.int32, x.shape, 0) % seg
    ok = (pos + d >= 0) & (pos + d < seg)
    return jnp.where(ok, rolled, jnp.zeros_like(x))


def _dwconv3(x, w, seg):
    return _shift_rows(x, -1, seg) * w[0:1] + x * w[1:2] + _shift_rows(x, 1, seg) * w[2:3]


def _dwconv_kernel(x_ref, w_ref, o_ref, *, seg):
    o_ref[...] = _dwconv3(x_ref[...], w_ref[...], seg)


def _hy_dwconv(rest, hy_short_w, layer, width):
    b, s, _ = rest.shape
    tn = 512
    return pl.pallas_call(
        functools.partial(_dwconv_kernel, seg=s), grid=(b, width // tn),
        in_specs=[pl.BlockSpec((None, s, tn), lambda b_, j: (b_, 0, j)),
                  pl.BlockSpec((None, 3, tn), lambda b_, j: (layer, 0, j))],
        out_specs=pl.BlockSpec((None, s, tn), lambda b_, j: (b_, 0, j)),
        out_shape=jax.ShapeDtypeStruct((b, s, width), F32),
        compiler_params=_params(2), name="hy_dwconv",
    )(rest, hy_short_w)


def _hdot(a, b):
    return jnp.dot(a, b, preferred_element_type=F32, precision=lax.Precision.HIGHEST)


def _filter_kernel(z_ref, w1_ref, b1_ref, w2_ref, b2_ref, w3f_ref, w3b_ref, fr_ref, dec_ref, e_ref, o_ref):
    fr = fr_ref[...]
    h = jnp.sin(fr[0:1] * (_hdot(z_ref[...], w1_ref[...]) + b1_ref[...]))
    h = jnp.sin(fr[1:2] * (_hdot(h, w2_ref[...]) + b2_ref[...]))
    dec = dec_ref[...]
    fwd = _hdot(h, w3f_ref[...]) * dec
    bwd = _hdot(h, w3b_ref[...]) * dec
    row = lax.broadcasted_iota(jnp.int32, bwd.shape, 0)
    bwd = jnp.where(row == 0, jnp.zeros_like(bwd), bwd)
    norm = jnp.sum(jnp.abs(fwd), axis=0, keepdims=True) + jnp.sum(jnp.abs(bwd), axis=0, keepdims=True) + EPS
    e_ref[...] = (fwd + bwd) / norm
    o_ref[...] = (fwd - bwd) / norm


def _hy_filters(z, dec, w1, b1, w2, b2, w3, freq, layer, width):
    s = z.shape[0]
    depth, emb, fd = w1.shape
    w1p = jnp.pad(w1, ((0, 0), (0, z.shape[1] - emb), (0, 0)))
    tc = 256
    nct = width // tc
    n_oc = HY_ORDER * nct
    return pl.pallas_call(
        _filter_kernel, grid=(n_oc,),
        in_specs=[pl.BlockSpec((s, z.shape[1]), lambda j: (0, 0)),
                  pl.BlockSpec((None, z.shape[1], fd), lambda j: (layer, 0, 0)),
                  pl.BlockSpec((None, 1, fd), lambda j: (layer, 0, 0)),
                  pl.BlockSpec((None, fd, fd), lambda j: (layer, 0, 0)),
                  pl.BlockSpec((None, 1, fd), lambda j: (layer, 0, 0)),
                  pl.BlockSpec((None, fd, tc), lambda j: (layer, 0, j)),
                  pl.BlockSpec((None, fd, tc), lambda j: (layer, 0, n_oc + j)),
                  pl.BlockSpec((None, 2, fd), lambda j: (layer, 0, 0)),
                  pl.BlockSpec((s, tc), lambda j: (0, j % nct))],
        out_specs=[pl.BlockSpec((s, tc), lambda j: (0, j))] * 2,
        out_shape=[jax.ShapeDtypeStruct((s, HY_ORDER * width), F32)] * 2,
        compiler_params=_params(1), name="hy_filter",
    )(z, w1p, b1.reshape(depth, 1, fd), w2, b2.reshape(depth, 1, fd), w3, w3, freq, dec)


def _spectrum_kernel(f_ref, e_ref, o_ref, k_ref, *, n_fft):
    f = f_ref[...]
    acc_e = _bdot(f, e_ref[...])
    acc_o = _bdot(f, o_ref[...])
    hb = f.shape[0] // 2
    first = lax.broadcasted_iota(jnp.int32, (hb, acc_e.shape[1]), 0) + pl.program_id(0) * hb == 0
    scale = jnp.where(first, 1.0 / n_fft, 2.0 / n_fft)
    k_ref[0:hb, :] = acc_e[:hb] * scale
    k_ref[hb:, :] = jnp.where(first, acc_e[hb:], acc_o[hb:]) * scale


def _hy_spectrum(fmat, e, o, *, tmf):
    n_fft, s = fmat.shape
    n = e.shape[1]
    tn = 512
    return pl.pallas_call(
        functools.partial(_spectrum_kernel, n_fft=n_fft), grid=(n_fft // tmf, n // tn),
        in_specs=[pl.BlockSpec((tmf, s), lambda i, j: (i, 0)),
                  pl.BlockSpec((s, tn), lambda i, j: (0, j)),
                  pl.BlockSpec((s, tn), lambda i, j: (0, j))],
        out_specs=pl.BlockSpec((tmf, tn), lambda i, j: (i, j)),
        out_shape=jax.ShapeDtypeStruct((n_fft, n), F32),
        compiler_params=_params(2), name="hy_spectrum",
    )(fmat, e, o)


def _fwd_dft_kernel(f_ref, u_ref, k_ref, y_ref):
    acc = _bdot(f_ref[...], u_ref[...])
    hb = acc.shape[0] // 2
    ua, ub = acc[:hb], acc[hb:]
    ka, kb = k_ref[0:hb, :], k_ref[hb:, :]
    first = lax.broadcasted_iota(jnp.int32, ua.shape, 0) + pl.program_id(0) * hb == 0
    ya = ua * ka - jnp.where(first, jnp.zeros_like(ub), ub * kb)
    yb = jnp.where(first, ub * kb, ua * kb + ub * ka)
    y_ref[0:hb, :] = ya.astype(y_ref.dtype)
    y_ref[hb:, :] = yb.astype(y_ref.dtype)


def _hy_fwd(fmat, u, ucol, kf, order, *, tmf):
    n_fft, s = fmat.shape
    b = u.shape[0]
    c = kf.shape[1] // HY_ORDER
    return pl.pallas_call(
        _fwd_dft_kernel, grid=(n_fft // tmf, b),
        in_specs=[pl.BlockSpec((tmf, s), lambda i, b_: (i, 0)),
                  pl.BlockSpec((None, s, c), lambda i, b_: (b_, 0, ucol)),
                  pl.BlockSpec((tmf, c), lambda i, b_: (i, order))],
        out_specs=pl.BlockSpec((None, tmf, c), lambda i, b_: (b_, i, 0)),
        out_shape=jax.ShapeDtypeStruct((b, n_fft, c), BF16),
        compiler_params=_params(2), name="hy_fwd",
    )(fmat, u, kf)


def _inv_dft_kernel(ft_ref, y_ref, u_ref, g_ref, bias_ref, o_ref):
    conv = jnp.dot(ft_ref[...], y_ref[...], preferred_element_type=F32)
    o_ref[...] = (g_ref[...] * (conv + u_ref[...] * bias_ref[...])).astype(o_ref.dtype)


def _hy_inv(fmat_t, yf, u, ucol, gate, gcol, hy_bias, bias_row, out_dtype, *, tmi):
    s, n_fft = fmat_t.shape
    b, _, c = yf.shape
    return pl.pallas_call(
        _inv_dft_kernel, grid=(s // tmi, b),
        in_specs=[pl.BlockSpec((tmi, n_fft), lambda i, b_: (i, 0)),
                  pl.BlockSpec((None, n_fft, c), lambda i, b_: (b_, 0, 0)),
                  pl.BlockSpec((None, tmi, c), lambda i, b_: (b_, i, ucol)),
                  pl.BlockSpec((None, tmi, c), lambda i, b_: (b_, i, gcol)),
                  pl.BlockSpec((None, 1, c), lambda i, b_: (bias_row, 0, 0))],
        out_specs=pl.BlockSpec((None, tmi, c), lambda i, b_: (b_, i, 0)),
        out_shape=jax.ShapeDtypeStruct((b, s, c), out_dtype),
        compiler_params=_params(2), name="hy_inv",
    )(fmat_t, yf, u, gate, hy_bias)


def _pool_kernel(x_ref, w_ref, sc_ref, o_ref):
    x = x_ref[...]
    s = x.shape[0]
    pos = lax.broadcasted_iota(jnp.int32, x.shape, 0)
    grp = pl.program_id(1)
    for gi, win in enumerate(POOL_WINDOWS):
        @pl.when(grp == gi)
        def _(win=win):
            half = win // 2
            tot = x
            for d in range(-half, win - half):
                if d != 0:
                    tot = tot + _shift_rows(x, d, s)
            cnt = jnp.minimum(pos + (win - half), s) - jnp.maximum(pos - half, 0)
            mean = tot / cnt.astype(F32)
            o_ref[...] = (_bdot(mean - x, w_ref[...]) * sc_ref[...]).astype(o_ref.dtype)


def _pool(rest, col_off, pool_w, pool_scale, layer):
    b, s, _ = rest.shape
    depth, groups, gd, _ = pool_w.shape
    off = col_off // gd
    return pl.pallas_call(
        _pool_kernel, grid=(b, groups),
        in_specs=[pl.BlockSpec((None, s, gd), lambda b_, g: (b_, 0, off + g)),
                  pl.BlockSpec((None, None, gd, gd), lambda b_, g: (layer, g, 0, 0)),
                  pl.BlockSpec((None, 1, gd), lambda b_, g: (layer, 0, g))],
        out_specs=pl.BlockSpec((None, s, gd), lambda b_, g: (b_, 0, g)),
        out_shape=jax.ShapeDtypeStruct((b, s, groups * gd), BF16),
        compiler_params=_params(2), name="pool",
    )(rest, pool_w, pool_scale.reshape(depth, 1, groups * gd))


def _merge_kernel(att_ref, hy_ref, po_ref, wa_ref, wh_ref, wp_ref, ga_ref, gh_ref, gp_ref, o_ref):
    acc = _sigmoid(ga_ref[...]) * _bdot(att_ref[...], wa_ref[...])
    acc = acc + _sigmoid(gh_ref[...]) * _bdot(hy_ref[...], wh_ref[...])
    acc = acc + _sigmoid(gp_ref[...]) * _bdot(po_ref[...], wp_ref[...])
    o_ref[...] = acc.astype(o_ref.dtype)


def _merge(att, hyo, poo, rest, gate_off, w_att_o, w_hy_o, w_pool_o, layer, *, tm, tn):
    r = att.shape[0]
    d = w_att_o.shape[2]
    goff = gate_off // tn
    nd = d // tn

    def a_spec(width):
        return pl.BlockSpec((tm, width), lambda i, j: (i, 0))

    def w_spec(width):
        return pl.BlockSpec((None, width, tn), lambda i, j: (layer, 0, j))

    def g_spec(branch):
        return pl.BlockSpec((tm, tn), lambda i, j: (i, goff + branch * nd + j))

    return pl.pallas_call(
        _merge_kernel, grid=(r // tm, nd),
        in_specs=[a_spec(att.shape[1]), a_spec(hyo.shape[1]), a_spec(poo.shape[1]),
                  w_spec(att.shape[1]), w_spec(hyo.shape[1]), w_spec(poo.shape[1]),
                  g_spec(0), g_spec(1), g_spec(2)],
        out_specs=pl.BlockSpec((tm, tn), lambda i, j: (i, j)),
        out_shape=jax.ShapeDtypeStruct((r, d), BF16),
        compiler_params=_params(2), name="merge",
    )(att, hyo, poo, w_att_o, w_hy_o, w_pool_o, rest, rest, rest)


def _gelu_tanh(x):
    return 0.5 * x * (1.0 + jnp.tanh(math.sqrt(2.0 / math.pi) * (x + 0.044715 * (x * x * x))))


def _ffn_up_kernel(a_ref, wg_ref, wv_ref, cw_ref, o_ref, *, seg):
    a = a_ref[...]
    gate = _dwconv3(_bdot(a, wg_ref[...]), cw_ref[...], seg)
    o_ref[...] = (_gelu_tanh(gate) * _bdot(a, wv_ref[...])).astype(o_ref.dtype)


def _ffn_up(h, w_up, conv_w, layer, seg, *, tn):
    g, s, k = h.shape
    ff = w_up.shape[2] // 2
    nt = ff // tn
    return pl.pallas_call(
        functools.partial(_ffn_up_kernel, seg=seg), grid=(g, nt),
        in_specs=[pl.BlockSpec((None, s, k), lambda a_, j: (a_, 0, 0)),
                  pl.BlockSpec((None, k, tn), lambda a_, j: (layer, 0, j)),
                  pl.BlockSpec((None, k, tn), lambda a_, j: (layer, 0, nt + j)),
                  pl.BlockSpec((None, 3, tn), lambda a_, j: (layer, 0, j))],
        out_specs=pl.BlockSpec((None, s, tn), lambda a_, j: (a_, 0, j)),
        out_shape=jax.ShapeDtypeStruct((g, s, ff), BF16),
        compiler_params=_params(2), name="ffn_up",
    )(h, w_up, w_up, conv_w)


def _rope_tables(seq):
    rows = seq // GRID_W
    row = jnp.repeat(jnp.arange(rows, dtype=F32), GRID_W)
    col = jnp.tile(jnp.arange(GRID_W, dtype=F32), rows)
    n_freq = ATT_HEAD_DIM // 4
    inv = ROPE_BASE ** (-jnp.arange(n_freq, dtype=F32) / n_freq)
    ang = jnp.stack([row[:, None] * inv, col[:, None] * inv], axis=1)
    cos, sin = jnp.cos(ang), jnp.sin(ang)
    zero = jnp.zeros_like(sin)
    co = jnp.stack([cos, cos], axis=2).reshape(seq, ATT_HEAD_DIM)
    sa = jnp.stack([-sin, zero], axis=2).reshape(seq, ATT_HEAD_DIM)
    sb = jnp.stack([zero, sin], axis=2).reshape(seq, ATT_HEAD_DIM)
    rep = LANE // ATT_HEAD_DIM
    return tuple(jnp.tile(t, (1, rep)) for t in (co, sa, sb))


def _dft_matrices(s, hb):
    n_fft = 2 * s
    lo = min(s, 64)
    hi = s // lo
    sp = jnp.arange(s, dtype=jnp.int32)[None, :]
    f_lo = jnp.arange(lo, dtype=jnp.int32)[:, None]
    f_hi = jnp.arange(hi, dtype=jnp.int32)[:, None] * lo
    ang_lo = (2.0 * math.pi / n_fft) * ((f_lo * sp) % n_fft).astype(F32)
    ang_hi = (2.0 * math.pi / n_fft) * ((f_hi * sp) % n_fft).astype(F32)
    c1, s1 = jnp.cos(ang_lo)[None], jnp.sin(ang_lo)[None]
    c2, s2 = jnp.cos(ang_hi)[:, None], jnp.sin(ang_hi)[:, None]
    cmat = (c1 * c2 - s1 * s2).reshape(s, s)
    smat = (s1 * c2 + c1 * s2).reshape(s, s)
    nyq = jnp.where(jnp.arange(s) % 2 == 0, 1.0, -1.0).astype(F32)
    smat = smat.at[0].set(nyq)
    fmat = jnp.concatenate([cmat.reshape(s // hb, hb, s), smat.reshape(s // hb, hb, s)], axis=1)
    fmat = fmat.reshape(n_fft, s).astype(BF16)
    return fmat, fmat.T


def _hy_tables(s, width):
    t = jnp.linspace(0.0, 1.0, s, dtype=F32)[:, None]
    w_ang = 2.0 * math.pi * jnp.arange(s, dtype=F32)[:, None] / s
    f = jnp.linspace(1e-4, HY_BANDS - 1, HY_BANDS, dtype=F32)[None, :]
    z = jnp.concatenate([t, jnp.cos(f * w_ang), -jnp.sin(f * w_ang)], axis=-1)
    z = jnp.pad(z, ((0, 0), (0, LANE - HY_EMB_DIM)))
    deltas = jnp.abs(jnp.linspace(HY_MIN_DECAY, HY_MAX_DECAY, width, dtype=F32))
    return z, jnp.exp(-t * deltas[None, :])


def _hyena_branch(rest_seq, layer, p, tabs):
    width = p["w_hy_o"].shape[1]
    z, dec, fmat, fmat_t, tmf, tmi = tabs
    conv = _hy_dwconv(rest_seq, p["hy_short_w"], layer, HY_ORDER * width + width)
    e, o = _hy_filters(z, dec, p["hy_ffn_w1"], p["hy_ffn_b1"], p["hy_ffn_w2"], p["hy_ffn_b2"],
                       p["hy_ffn_w3"], p["hy_freq"], layer, width)
    kf = _hy_spectrum(fmat, e, o, tmf=tmf)
    depth = p["hy_bias"].shape[0]
    bias = p["hy_bias"].reshape(depth * HY_ORDER, 1, width)
    yf = _hy_fwd(fmat, conv, 0, kf, 0, tmf=tmf)
    zsig = _hy_inv(fmat_t, yf, conv, 0, conv, 1, bias, layer * HY_ORDER, F32, tmi=tmi)
    yf = _hy_fwd(fmat, zsig, 0, kf, 1, tmf=tmf)
    return _hy_inv(fmat_t, yf, zsig, 0, conv, 2, bias, layer * HY_ORDER + 1, BF16, tmi=tmi)


def kernel(x, c, ctx, c_ctx, w_ada, b_ada, norm_g, w_in, diff_lam, attn_subln_g, hy_short_w,
           hy_ffn_w1, hy_ffn_b1, hy_ffn_w2, hy_ffn_b2, hy_ffn_w3, hy_freq, hy_bias, pool_w, pool_scale,
           w_att_o, w_hy_o, w_pool_o, w_out, w_up, ff_conv_w, w_down):
    p = dict(hy_short_w=hy_short_w, hy_ffn_w1=hy_ffn_w1, hy_ffn_b1=hy_ffn_b1, hy_ffn_w2=hy_ffn_w2,
             hy_ffn_b2=hy_ffn_b2, hy_ffn_w3=hy_ffn_w3, hy_freq=hy_freq, hy_bias=hy_bias, w_hy_o=w_hy_o)
    b, seq, d = x.shape
    lc = ctx.shape[1]
    depth = w_ada.shape[0]
    att_w = w_att_o.shape[1]
    hy_w = w_hy_o.shape[1]
    pool_width = w_pool_o.shape[1]
    qkv_w = 3 * att_w
    rest_w = w_in.shape[2] - qkv_w
    gate_off = 3 * hy_w + pool_width

    cond = jnp.concatenate([c, c_ctx[None], jnp.zeros((8 - b - 1, d), F32)], axis=0)
    mod = _ada(cond, w_ada, b_ada)

    rope = _rope_tables(seq)
    tabs_x = _hy_tables(seq, hy_w) + _dft_matrices(seq, 512) + (1024, 1024)
    tabs_c = _hy_tables(lc, hy_w) + _dft_matrices(lc, lc) + (2 * lc, lc)

    ctx = ctx.reshape(1, b * lc, d)

    def mods(layer):
        m = mod[layer].reshape(8, 6, d)
        return [m[:b, k][:, None, :] for k in range(6)], [m[b:b + 1, k][:, None, :] for k in range(6)]

    m_x, m_c = mods(0)
    hx = _resnorm(x, g_norm=norm_g[0, 0], shift=m_x[0], scale=m_x[1])
    hc = _resnorm(ctx, g_norm=norm_g[0, 0], shift=m_c[0], scale=m_c[1])

    for l in range(depth):
        last = l == depth - 1
        lam_init = 0.8 - 0.6 * math.exp(-0.3 * l)
        m_x, m_c = mods(l)
        streams = [("x", x, hx, m_x, seq, b), ("c", ctx, hc, m_c, lc, b)]
        qkv_c = _mm(hc, w_in, l, 0, qkv_w, BF16, tm=b * lc, tn=512, name="mm_qkv").reshape(b, lc, qkv_w)
        qkv_x = _mm(hx, w_in, l, 0, qkv_w, BF16, tm=seq, tn=512, rope=rope, rope_cols=2 * att_w,
                    name="mm_qkv_rope")
        new = {}
        for tag, res, h, m, s_len, nb in streams:
            is_x = tag == "x"
            if last and not is_x:
                continue
            g_rows = res.shape[0] * res.shape[1]
            rest = _mm(h, w_in, l, qkv_w, rest_w, F32, tm=h.shape[1], tn=512, name="mm_rest")
            rest_seq = rest.reshape(nb, s_len, rest_w)
            if is_x:
                att = _attention(qkv_x, qkv_c, diff_lam, attn_subln_g, l, lam_init, tq=256)
            else:
                att = _attention(qkv_c, None, diff_lam, attn_subln_g, l, lam_init, tq=s_len)
            hyo = _hyena_branch(rest_seq, l, p, tabs_x if is_x else tabs_c)
            poo = _pool(rest_seq, 3 * hy_w, pool_w, pool_scale, l)
            merged = _merge(att.reshape(g_rows, att_w), hyo.reshape(g_rows, hy_w),
                            poo.reshape(g_rows, pool_width), rest.reshape(g_rows, rest_w), gate_off,
                            w_att_o, w_hy_o, w_pool_o, l, tm=1024, tn=512)
            merged = merged.reshape(h.shape)
            mix = _mm(merged, w_out, l, 0, d, F32, tm=h.shape[1], tn=512, name="mm_out")
            res, h2 = _resnorm(res, mix, norm_g[l, 1], m[2], norm_g[l, 2], m[3], m[4])
            act = _ffn_up(h2, w_up, ff_conv_w, l, s_len, tn=256)
            ffn = _mm_acc(act.reshape(g_rows, act.shape[2]), w_down, l, tm=1024, tk=512).reshape(res.shape)
            if last:
                res = _resnorm(res, ffn, norm_g[l, 3], m[5])
                h_next = None
            else:
                m_nx, m_nc = mods(l + 1)
                m_n = m_nx if is_x else m_nc
                res, h_next = _resnorm(res, ffn, norm_g[l, 3], m[5], norm_g[l + 1, 0], m_n[0], m_n[1])
            new[tag] = (res, h_next)
        x, hx = new["x"]
        if not last:
            ctx, hc = new["c"]
    return x
```

```python
import functools
import math

import jax
import jax.numpy as jnp
from jax import lax
from jax.experimental import pallas as pl
from jax.experimental.pallas import tpu as pltpu

F32 = jnp.float32
BF16 = jnp.bfloat16

GRID_W = 64
N_BRANCH = 3
ATT_HEAD_DIM = 64
ATT_V_DIM = 2 * ATT_HEAD_DIM
ROPE_BASE = 10000.0
HY_ORDER = 2
HY_BANDS = 16
HY_EMB_DIM = 1 + 2 * HY_BANDS
HY_MIN_DECAY = math.log(1e-2) / 1.5
HY_MAX_DECAY = math.log(1e-2) / 0.3
POOL_WINDOWS = (2, 4, 8, 16)
EPS = 1e-6

V7X_VMEM_BYTES = 64 * 1024 * 1024
VMEM_LIMIT = V7X_VMEM_BYTES - 8 * 1024 * 1024
LANE = 128


def _params(n_axes):
    return pltpu.CompilerParams(dimension_semantics=("arbitrary",) * n_axes,
                                vmem_limit_bytes=VMEM_LIMIT)


def _bdot(a, b):
    return jnp.dot(a.astype(BF16), b.astype(BF16), preferred_element_type=F32)


def _sigmoid(x):
    return 1.0 / (1.0 + jnp.exp(-x))


def _rms(x, g):
    return x * lax.rsqrt(jnp.mean(x * x, axis=-1, keepdims=True) + EPS) * g


def _ada_kernel(c_ref, w_ref, b_ref, o_ref):
    c = c_ref[...]
    o_ref[...] = _bdot(c * _sigmoid(c), w_ref[...]) + b_ref[...]


def _ada(cond, w_ada, b_ada):
    depth, d, n = w_ada.shape
    rows = cond.shape[0]
    tn = 1024
    return pl.pallas_call(
        _ada_kernel,
        grid=(depth, n // tn),
        in_specs=[pl.BlockSpec((rows, d), lambda l, j: (0, 0)),
                  pl.BlockSpec((None, d, tn), lambda l, j: (l, 0, j)),
                  pl.BlockSpec((None, 1, tn), lambda l, j: (l, 0, j))],
        out_specs=pl.BlockSpec((None, rows, tn), lambda l, j: (l, 0, j)),
        out_shape=jax.ShapeDtypeStruct((depth, rows, n), F32),
        compiler_params=_params(2),
        name="ada",
    )(cond, w_ada, b_ada.reshape(depth, 1, n))


def _resnorm_kernel(*refs, has_res, has_norm):
    it = iter(refs)
    x_ref = next(it)
    if has_res:
        y_ref, gres_ref, gate_ref = next(it), next(it), next(it)
    if has_norm:
        gnorm_ref, shift_ref, scale_ref = next(it), next(it), next(it)
    x = x_ref[...]
    if has_res:
        x = x + gate_ref[...] * _rms(y_ref[...], gres_ref[...])
        xo_ref = next(it)
        xo_ref[...] = x
    if has_norm:
        h_ref = next(it)
        h_ref[...] = (_rms(x, gnorm_ref[...]) * (1.0 + scale_ref[...]) + shift_ref[...]).astype(BF16)


def _resnorm(x, y=None, g_res=None, gate=None, g_norm=None, shift=None, scale=None):
    g, s, d = x.shape
    ts = 256
    has_res, has_norm = y is not None, g_norm is not None
    row = pl.BlockSpec((None, ts, d), lambda a, i: (a, i, 0))
    vec = pl.BlockSpec((1, d), lambda a, i: (0, 0))
    mod = pl.BlockSpec((None, 1, d), lambda a, i: (a, 0, 0))
    args, specs, outs, out_specs = [x], [row], [], []
    if has_res:
        args += [y, g_res.reshape(1, d), gate]
        specs += [row, vec, mod]
        outs.append(jax.ShapeDtypeStruct((g, s, d), F32))
        out_specs.append(row)
    if has_norm:
        args += [g_norm.reshape(1, d), shift, scale]
        specs += [vec, mod, mod]
        outs.append(jax.ShapeDtypeStruct((g, s, d), BF16))
        out_specs.append(row)
    res = pl.pallas_call(
        functools.partial(_resnorm_kernel, has_res=has_res, has_norm=has_norm),
        grid=(g, s // ts), in_specs=specs, out_specs=out_specs, out_shape=outs,
        compiler_params=_params(2), name="resnorm",
    )(*args)
    return res if len(res) > 1 else res[0]


def _mm_kernel(a_ref, w_ref, o_ref):
    o_ref[...] = _bdot(a_ref[...], w_ref[...]).astype(o_ref.dtype)


def _mm_rope_kernel(a_ref, w_ref, co_ref, sa_ref, sb_ref, o_ref, *, rope_tiles):
    acc = _bdot(a_ref[...], w_ref[...])
    j = pl.program_id(2)

    @pl.when(j < rope_tiles)
    def _():
        co, sa, sb = co_ref[...], sa_ref[...], sb_ref[...]
        quarter = ATT_HEAD_DIM // 4
        for c in range(acc.shape[1] // LANE):
            t = acc[:, c * LANE:(c + 1) * LANE]
            r = t * co + pltpu.roll(t, LANE - quarter, 1) * sa + pltpu.roll(t, quarter, 1) * sb
            o_ref[:, c * LANE:(c + 1) * LANE] = r.astype(o_ref.dtype)

    @pl.when(j >= rope_tiles)
    def _():
        o_ref[...] = acc.astype(o_ref.dtype)


def _mm(a, w, layer, col_off, n, out_dtype, *, tm, tn, rope=None, rope_cols=0, name="mm"):
    g, s, k = a.shape
    assert col_off % tn == 0 and n % tn == 0 and s % tm == 0
    off = col_off // tn
    in_specs = [pl.BlockSpec((None, tm, k), lambda a_, i, j: (a_, i, 0)),
                pl.BlockSpec((None, k, tn), lambda a_, i, j: (layer, 0, off + j))]
    args = [a, w]
    if rope is None:
        kern = _mm_kernel
    else:
        kern = functools.partial(_mm_rope_kernel, rope_tiles=rope_cols // tn)
        in_specs += [pl.BlockSpec((tm, LANE), lambda a_, i, j: (i, 0))] * 3
        args += list(rope)
    return pl.pallas_call(
        kern, grid=(g, s // tm, n // tn), in_specs=in_specs,
        out_specs=pl.BlockSpec((None, tm, tn), lambda a_, i, j: (a_, i, j)),
        out_shape=jax.ShapeDtypeStruct((g, s, n), out_dtype),
        compiler_params=_params(3), name=name,
    )(*args)


def _attn_kernel(*refs, has_ctx, lam_init, s_own, s_ctx, n_sub):
    if has_ctx:
        q_ref, k_ref, v_ref, kc_ref, vc_ref, lam_ref, g_ref, o_ref, kk, vv = refs
    else:
        q_ref, k_ref, v_ref, lam_ref, g_ref, o_ref, kk, vv = refs

    @pl.when(pl.program_id(2) == 0)
    def _():
        kk[0:s_own, :] = k_ref[...]
        vv[0:s_own, :] = v_ref[...]
        if has_ctx:
            kk[s_own:s_own + s_ctx, :] = kc_ref[...]
            vv[s_own:s_own + s_ctx, :] = vc_ref[...]

    lp = lam_ref[...]
    lam = (jnp.exp(jnp.sum(lp[0:1] * lp[1:2], axis=-1, keepdims=True))
           - jnp.exp(jnp.sum(lp[2:3] * lp[3:4], axis=-1, keepdims=True)) + lam_init)

    ts = q_ref.shape[0] // n_sub
    for sb in range(n_sub):
        q = q_ref[sb * ts:(sb + 1) * ts, :] * (ATT_HEAD_DIM ** -0.5)
        lane = lax.broadcasted_iota(jnp.int32, q.shape, 1)
        zero = jnp.zeros_like(q)
        q2 = jnp.concatenate([jnp.where(lane < ATT_HEAD_DIM, q, zero),
                              jnp.where(lane >= ATT_HEAD_DIM, q, zero)], axis=0)
        s = lax.dot_general(q2, kk[...], (((1,), (1,)), ((), ())), preferred_element_type=F32)
        m = jnp.max(s, axis=-1, keepdims=True)
        e = jnp.exp(s - m)
        den = jnp.sum(e, axis=-1, keepdims=True)
        o = jnp.dot(e.astype(BF16), vv[...], preferred_element_type=F32) * (1.0 / den)
        att = o[:ts] - lam * o[ts:]
        o_ref[sb * ts:(sb + 1) * ts, :] = (_rms(att, g_ref[...]) * (1.0 - lam_init)).astype(o_ref.dtype)


def _attention(qkv, qkv_ctx, diff_lam, subln_g, layer, lam_init, *, tq, n_sub):
    b, s, w3 = qkv.shape
    hd = ATT_V_DIM
    heads = w3 // (3 * hd)
    has_ctx = qkv_ctx is not None
    s_ctx = qkv_ctx.shape[1] if has_ctx else 0
    in_specs = [pl.BlockSpec((None, tq, hd), lambda b_, h, i: (b_, i, h)),
                pl.BlockSpec((None, s, hd), lambda b_, h, i: (b_, 0, heads + h)),
                pl.BlockSpec((None, s, hd), lambda b_, h, i: (b_, 0, 2 * heads + h))]
    args = [qkv, qkv, qkv]
    if has_ctx:
        in_specs += [pl.BlockSpec((None, s_ctx, hd), lambda b_, h, i: (b_, 0, heads + h)),
                     pl.BlockSpec((None, s_ctx, hd), lambda b_, h, i: (b_, 0, 2 * heads + h))]
        args += [qkv_ctx, qkv_ctx]
    in_specs += [pl.BlockSpec((None, 4, ATT_HEAD_DIM), lambda b_, h, i: (layer, 0, 0)),
                 pl.BlockSpec((None, 1, hd), lambda b_, h, i: (layer, 0, 0))]
    args += [diff_lam, subln_g.reshape(subln_g.shape[0], 1, hd)]
    return pl.pallas_call(
        functools.partial(_attn_kernel, has_ctx=has_ctx, lam_init=lam_init, s_own=s, s_ctx=s_ctx,
                          n_sub=n_sub),
        grid=(b, heads, s // tq), in_specs=in_specs,
        out_specs=pl.BlockSpec((None, tq, hd), lambda b_, h, i: (b_, i, h)),
        out_shape=jax.ShapeDtypeStruct((b, s, heads * hd), BF16),
        scratch_shapes=[pltpu.VMEM((s + s_ctx, hd), BF16), pltpu.VMEM((s + s_ctx, hd), BF16)],
        compiler_params=_params(3), name="attn",
    )(*args)


def _shift_rows(x, d, seg):
    n = x.shape[0]
    if d == 0:
        return x
    rolled = pltpu.roll(x, (-d) % n, 0)
    pos = lax.broadcasted_iota(jnp.int32, x.shape, 0) % seg
    ok = (pos + d >= 0) & (pos + d < seg)
    return jnp.where(ok, rolled, jnp.zeros_like(x))


def _dwconv3(x, w, seg):
    return _shift_rows(x, -1, seg) * w[0:1] + x * w[1:2] + _shift_rows(x, 1, seg) * w[2:3]


def _dwconv_kernel(x_ref, w_ref, o_ref, *, seg):
    o_ref[...] = _dwconv3(x_ref[...], w_ref[...], seg)


def _hy_dwconv(rest, hy_short_w, layer, width):
    b, s, _ = rest.shape
    tn = 512
    return pl.pallas_call(
        functools.partial(_dwconv_kernel, seg=s), grid=(b, width // tn),
        in_specs=[pl.BlockSpec((None, s, tn), lambda b_, j: (b_, 0, j)),
                  pl.BlockSpec((None, 3, tn), lambda b_, j: (layer, 0, j))],
        out_specs=pl.BlockSpec((None, s, tn), lambda b_, j: (b_, 0, j)),
        out_shape=jax.ShapeDtypeStruct((b, s, width), F32),
        compiler_params=_params(2), name="hy_dwconv",
    )(rest, hy_short_w)


def _hdot(a, b):
    return jnp.dot(a, b, preferred_element_type=F32, precision=lax.Precision.HIGHEST)


def _filter_kernel(z_ref, w1_ref, b1_ref, w2_ref, b2_ref, w3f_ref, w3b_ref, fr_ref, dec_ref, e_ref, o_ref):
    fr = fr_ref[...]
    h = jnp.sin(fr[0:1] * (_hdot(z_ref[...], w1_ref[...]) + b1_ref[...]))
    h = jnp.sin(fr[1:2] * (_hdot(h, w2_ref[...]) + b2_ref[...]))
    dec = dec_ref[...]
    fwd = _hdot(h, w3f_ref[...]) * dec
    bwd = _hdot(h, w3b_ref[...]) * dec
    row = lax.broadcasted_iota(jnp.int32, bwd.shape, 0)
    bwd = jnp.where(row == 0, jnp.zeros_like(bwd), bwd)
    norm = jnp.sum(jnp.abs(fwd), axis=0, keepdims=True) + jnp.sum(jnp.abs(bwd), axis=0, keepdims=True) + EPS
    e_ref[...] = (fwd + bwd) / norm
    o_ref[...] = (fwd - bwd) / norm


def _hy_filters(z, dec, w1, b1, w2, b2, w3, freq, layer, width):
    s = z.shape[0]
    depth, emb, fd = w1.shape
    w1p = jnp.pad(w1, ((0, 0), (0, z.shape[1] - emb), (0, 0)))
    tc = 256
    nct = width // tc
    n_oc = HY_ORDER * nct
    return pl.pallas_call(
        _filter_kernel, grid=(n_oc,),
        in_specs=[pl.BlockSpec((s, z.shape[1]), lambda j: (0, 0)),
                  pl.BlockSpec((None, z.shape[1], fd), lambda j: (layer, 0, 0)),
                  pl.BlockSpec((None, 1, fd), lambda j: (layer, 0, 0)),
                  pl.BlockSpec((None, fd, fd), lambda j: (layer, 0, 0)),
                  pl.BlockSpec((None, 1, fd), lambda j: (layer, 0, 0)),
                  pl.BlockSpec((None, fd, tc), lambda j: (layer, 0, j)),
                  pl.BlockSpec((None, fd, tc), lambda j: (layer, 0, n_oc + j)),
                  pl.BlockSpec((None, 2, fd), lambda j: (layer, 0, 0)),
                  pl.BlockSpec((s, tc), lambda j: (0, j % nct))],
        out_specs=[pl.BlockSpec((s, tc), lambda j: (0, j))] * 2,
        out_shape=[jax.ShapeDtypeStruct((s, HY_ORDER * width), F32)] * 2,
        compiler_params=_params(1), name="hy_filter",
    )(z, w1p, b1.reshape(depth, 1, fd), w2, b2.reshape(depth, 1, fd), w3, w3, freq, dec)


def _spectrum_kernel(f_ref, e_ref, o_ref, k_ref, *, n_fft):
    f = f_ref[...]
    acc_e = _bdot(f, e_ref[...])
    acc_o = _bdot(f, o_ref[...])
    hb = f.shape[0] // 2
    first = lax.broadcasted_iota(jnp.int32, (hb, acc_e.shape[1]), 0) + pl.program_id(0) * hb == 0
    scale = jnp.where(first, 1.0 / n_fft, 2.0 / n_fft)
    k_ref[0:hb, :] = acc_e[:hb] * scale
    k_ref[hb:, :] = jnp.where(first, acc_e[hb:], acc_o[hb:]) * scale


def _hy_spectrum(fmat, e, o, *, tmf):
    n_fft, s = fmat.shape
    n = e.shape[1]
    tn = 512
    return pl.pallas_call(
        functools.partial(_spectrum_kernel, n_fft=n_fft), grid=(n_fft // tmf, n // tn),
        in_specs=[pl.BlockSpec((tmf, s), lambda i, j: (i, 0)),
                  pl.BlockSpec((s, tn), lambda i, j: (0, j)),
                  pl.BlockSpec((s, tn), lambda i, j: (0, j))],
        out_specs=pl.BlockSpec((tmf, tn), lambda i, j: (i, j)),
        out_shape=jax.ShapeDtypeStruct((n_fft, n), F32),
        compiler_params=_params(2), name="hy_spectrum",
    )(fmat, e, o)


def _fwd_dft_kernel(f_ref, u_ref, k_ref, y_ref):
    acc = _bdot(f_ref[...], u_ref[...])
    hb = acc.shape[0] // 2
    ua, ub = acc[:hb], acc[hb:]
    ka, kb = k_ref[0:hb, :], k_ref[hb:, :]
    first = lax.broadcasted_iota(jnp.int32, ua.shape, 0) + pl.program_id(0) * hb == 0
    ya = ua * ka - jnp.where(first, jnp.zeros_like(ub), ub * kb)
    yb = jnp.where(first, ub * kb, ua * kb + ub * ka)
    y_ref[0:hb, :] = ya.astype(y_ref.dtype)
    y_ref[hb:, :] = yb.astype(y_ref.dtype)


def _hy_fwd(fmat, u, ucol, kf, order, *, tmf):
    n_fft, s = fmat.shape
    b = u.shape[0]
    c = kf.shape[1] // HY_ORDER
    return pl.pallas_call(
        _fwd_dft_kernel, grid=(n_fft // tmf, b),
        in_specs=[pl.BlockSpec((tmf, s), lambda i, b_: (i, 0)),
                  pl.BlockSpec((None, s, c), lambda i, b_: (b_, 0, ucol)),
                  pl.BlockSpec((tmf, c), lambda i, b_: (i, order))],
        out_specs=pl.BlockSpec((None, tmf, c), lambda i, b_: (b_, i, 0)),
        out_shape=jax.ShapeDtypeStruct((b, n_fft, c), BF16),
        compiler_params=_params(2), name="hy_fwd",
    )(fmat, u, kf)


def _inv_dft_kernel(ft_ref, y_ref, u_ref, g_ref, bias_ref, o_ref):
    conv = jnp.dot(ft_ref[...], y_ref[...], preferred_element_type=F32)
    o_ref[...] = (g_ref[...] * (conv + u_ref[...] * bias_ref[...])).astype(o_ref.dtype)


def _hy_inv(fmat_t, yf, u, ucol, gate, gcol, hy_bias, bias_row, out_dtype, *, tmi):
    s, n_fft = fmat_t.shape
    b, _, c = yf.shape
    return pl.pallas_call(
        _inv_dft_kernel, grid=(s // tmi, b),
        in_specs=[pl.BlockSpec((tmi, n_fft), lambda i, b_: (i, 0)),
                  pl.BlockSpec((None, n_fft, c), lambda i, b_: (b_, 0, 0)),
                  pl.BlockSpec((None, tmi, c), lambda i, b_: (b_, i, ucol)),
                  pl.BlockSpec((None, tmi, c), lambda i, b_: (b_, i, gcol)),
                  pl.BlockSpec((None, 1, c), lambda i, b_: (bias_row, 0, 0))],
        out_specs=pl.BlockSpec((None, tmi, c), lambda i, b_: (b_, i, 0)),
        out_shape=jax.ShapeDtypeStruct((b, s, c), out_dtype),
        compiler_params=_params(2), name="hy_inv",
    )(fmat_t, yf, u, gate, hy_bias)


def _pool_kernel(x_ref, w_ref, sc_ref, o_ref):
    x = x_ref[...]
    s = x.shape[0]
    pos = lax.broadcasted_iota(jnp.int32, x.shape, 0)
    grp = pl.program_id(1)
    for gi, win in enumerate(POOL_WINDOWS):
        @pl.when(grp == gi)
        def _(win=win):
            half = win // 2
            tot = x
            for d in range(-half, win - half):
                if d != 0:
                    tot = tot + _shift_rows(x, d, s)
            cnt = jnp.minimum(pos + (win - half), s) - jnp.maximum(pos - half, 0)
            mean = tot / cnt.astype(F32)
            o_ref[...] = (_bdot(mean - x, w_ref[...]) * sc_ref[...]).astype(o_ref.dtype)


def _pool(rest, col_off, pool_w, pool_scale, layer):
    b, s, _ = rest.shape
    depth, groups, gd, _ = pool_w.shape
    off = col_off // gd
    return pl.pallas_call(
        _pool_kernel, grid=(b, groups),
        in_specs=[pl.BlockSpec((None, s, gd), lambda b_, g: (b_, 0, off + g)),
                  pl.BlockSpec((None, None, gd, gd), lambda b_, g: (layer, g, 0, 0)),
                  pl.BlockSpec((None, 1, gd), lambda b_, g: (layer, 0, g))],
        out_specs=pl.BlockSpec((None, s, gd), lambda b_, g: (b_, 0, g)),
        out_shape=jax.ShapeDtypeStruct((b, s, groups * gd), BF16),
        compiler_params=_params(2), name="pool",
    )(rest, pool_w, pool_scale.reshape(depth, 1, groups * gd))


def _merge_kernel(att_ref, hy_ref, po_ref, wa_ref, wh_ref, wp_ref, ga_ref, gh_ref, gp_ref, o_ref):
    acc = _sigmoid(ga_ref[...]) * _bdot(att_ref[...], wa_ref[...])
    acc = acc + _sigmoid(gh_ref[...]) * _bdot(hy_ref[...], wh_ref[...])
    acc = acc + _sigmoid(gp_ref[...]) * _bdot(po_ref[...], wp_ref[...])
    o_ref[...] = acc.astype(o_ref.dtype)


def _merge(att, hyo, poo, rest, gate_off, w_att_o, w_hy_o, w_pool_o, layer, *, tm, tn):
    r = att.shape[0]
    d = w_att_o.shape[2]
    goff = gate_off // tn
    nd = d // tn

    def a_spec(width):
        return pl.BlockSpec((tm, width), lambda i, j: (i, 0))

    def w_spec(width):
        return pl.BlockSpec((None, width, tn), lambda i, j: (layer, 0, j))

    def g_spec(branch):
        return pl.BlockSpec((tm, tn), lambda i, j: (i, goff + branch * nd + j))

    return pl.pallas_call(
        _merge_kernel, grid=(r // tm, nd),
        in_specs=[a_spec(att.shape[1]), a_spec(hyo.shape[1]), a_spec(poo.shape[1]),
                  w_spec(att.shape[1]), w_spec(hyo.shape[1]), w_spec(poo.shape[1]),
                  g_spec(0), g_spec(1), g_spec(2)],
        out_specs=pl.BlockSpec((tm, tn), lambda i, j: (i, j)),
        out_shape=jax.ShapeDtypeStruct((r, d), BF16),
        compiler_params=_params(2), name="merge",
    )(att, hyo, poo, w_att_o, w_hy_o, w_pool_o, rest, rest, rest)


def _gelu_tanh(x):
    return 0.5 * x * (1.0 + jnp.tanh(math.sqrt(2.0 / math.pi) * (x + 0.044715 * (x * x * x))))


def _ffn_up_kernel(a_ref, wg_ref, wv_ref, cw_ref, o_ref, *, seg):
    a = a_ref[...]
    gate = _dwconv3(_bdot(a, wg_ref[...]), cw_ref[...], seg)
    o_ref[...] = (_gelu_tanh(gate) * _bdot(a, wv_ref[...])).astype(o_ref.dtype)


def _ffn_up(h, w_up, conv_w, layer, seg, *, tn):
    g, s, k = h.shape
    ff = w_up.shape[2] // 2
    nt = ff // tn
    return pl.pallas_call(
        functools.partial(_ffn_up_kernel, seg=seg), grid=(g, nt),
        in_specs=[pl.BlockSpec((None, s, k), lambda a_, j: (a_, 0, 0)),
                  pl.BlockSpec((None, k, tn), lambda a_, j: (layer, 0, j)),
                  pl.BlockSpec((None, k, tn), lambda a_, j: (layer, 0, nt + j)),
                  pl.BlockSpec((None, 3, tn), lambda a_, j: (layer, 0, j))],
        out_specs=pl.BlockSpec((None, s, tn), lambda a_, j: (a_, 0, j)),
        out_shape=jax.ShapeDtypeStruct((g, s, ff), BF16),
        compiler_params=_params(2), name="ffn_up",
    )(h, w_up, w_up, conv_w)


def _rope_tables(seq):
    rows = seq // GRID_W
    row = jnp.repeat(jnp.arange(rows, dtype=F32), GRID_W)
    col = jnp.tile(jnp.arange(GRID_W, dtype=F32), rows)
    n_freq = ATT_HEAD_DIM // 4
    inv = ROPE_BASE ** (-jnp.arange(n_freq, dtype=F32) / n_freq)
    ang = jnp.stack([row[:, None] * inv, col[:, None] * inv], axis=1)
    cos, sin = jnp.cos(ang), jnp.sin(ang)
    zero = jnp.zeros_like(sin)
    co = jnp.stack([cos, cos], axis=2).reshape(seq, ATT_HEAD_DIM)
    sa = jnp.stack([-sin, zero], axis=2).reshape(seq, ATT_HEAD_DIM)
    sb = jnp.stack([zero, sin], axis=2).reshape(seq, ATT_HEAD_DIM)
    rep = LANE // ATT_HEAD_DIM
    return tuple(jnp.tile(t, (1, rep)) for t in (co, sa, sb))


def _dft_matrices(s, hb):
    n_fft = 2 * s
    lo = min(s, 64)
    hi = s // lo
    sp = jnp.arange(s, dtype=jnp.int32)[None, :]
    f_lo = jnp.arange(lo, dtype=jnp.int32)[:, None]
    f_hi = jnp.arange(hi, dtype=jnp.int32)[:, None] * lo
    ang_lo = (2.0 * math.pi / n_fft) * ((f_lo * sp) % n_fft).astype(F32)
    ang_hi = (2.0 * math.pi / n_fft) * ((f_hi * sp) % n_fft).astype(F32)
    c1, s1 = jnp.cos(ang_lo)[None], jnp.sin(ang_lo)[None]
    c2, s2 = jnp.cos(ang_hi)[:, None], jnp.sin(ang_hi)[:, None]
    cmat = (c1 * c2 - s1 * s2).reshape(s, s)
    smat = (s1 * c2 + c1 * s2).reshape(s, s)
    nyq = jnp.where(jnp.arange(s) % 2 == 0, 1.0, -1.0).astype(F32)
    smat = smat.at[0].set(nyq)
    fmat = jnp.concatenate([cmat.reshape(s // hb, hb, s), smat.reshape(s // hb, hb, s)], axis=1)
    fmat = fmat.reshape(n_fft, s).astype(BF16)
    return fmat, fmat.T


def _hy_tables(s, width):
    t = jnp.linspace(0.0, 1.0, s, dtype=F32)[:, None]
    w_ang = 2.0 * math.pi * jnp.arange(s, dtype=F32)[:, None] / s
    f = jnp.linspace(1e-4, HY_BANDS - 1, HY_BANDS, dtype=F32)[None, :]
    z = jnp.concatenate([t, jnp.cos(f * w_ang), -jnp.sin(f * w_ang)], axis=-1)
    z = jnp.pad(z, ((0, 0), (0, LANE - HY_EMB_DIM)))
    deltas = jnp.abs(jnp.linspace(HY_MIN_DECAY, HY_MAX_DECAY, width, dtype=F32))
    return z, jnp.exp(-t * deltas[None, :])


def _hyena_branch(rest_seq, layer, p, tabs):
    width = p["w_hy_o"].shape[1]
    z, dec, fmat, fmat_t, tmf, tmi = tabs
    conv = _hy_dwconv(rest_seq, p["hy_short_w"], layer, HY_ORDER * width + width)
    e, o = _hy_filters(z, dec, p["hy_ffn_w1"], p["hy_ffn_b1"], p["hy_ffn_w2"], p["hy_ffn_b2"],
                       p["hy_ffn_w3"], p["hy_freq"], layer, width)
    kf = _hy_spectrum(fmat, e, o, tmf=tmf)
    depth = p["hy_bias"].shape[0]
    bias = p["hy_bias"].reshape(depth * HY_ORDER, 1, width)
    yf = _hy_fwd(fmat, conv, 0, kf, 0, tmf=tmf)
    zsig = _hy_inv(fmat_t, yf, conv, 0, conv, 1, bias, layer * HY_ORDER, F32, tmi=tmi)
    yf = _hy_fwd(fmat, zsig, 0, kf, 1, tmf=tmf)
    return _hy_inv(fmat_t, yf, zsig, 0, conv, 2, bias, layer * HY_ORDER + 1, BF16, tmi=tmi)


def kernel(x, c, ctx, c_ctx, w_ada, b_ada, norm_g, w_in, diff_lam, attn_subln_g, hy_short_w,
           hy_ffn_w1, hy_ffn_b1, hy_ffn_w2, hy_ffn_b2, hy_ffn_w3, hy_freq, hy_bias, pool_w, pool_scale,
           w_att_o, w_hy_o, w_pool_o, w_out, w_up, ff_conv_w, w_down):
    p = dict(hy_short_w=hy_short_w, hy_ffn_w1=hy_ffn_w1, hy_ffn_b1=hy_ffn_b1, hy_ffn_w2=hy_ffn_w2,
             hy_ffn_b2=hy_ffn_b2, hy_ffn_w3=hy_ffn_w3, hy_freq=hy_freq, hy_bias=hy_bias, w_hy_o=w_hy_o)
    b, seq, d = x.shape
    lc = ctx.shape[1]
    depth = w_ada.shape[0]
    att_w = w_att_o.shape[1]
    hy_w = w_hy_o.shape[1]
    pool_width = w_pool_o.shape[1]
    qkv_w = 3 * att_w
    rest_w = w_in.shape[2] - qkv_w
    gate_off = 3 * hy_w + pool_width

    cond = jnp.concatenate([c, c_ctx[None], jnp.zeros((8 - b - 1, d), F32)], axis=0)
    mod = _ada(cond, w_ada, b_ada)

    rope = _rope_tables(seq)
    tabs_x = _hy_tables(seq, hy_w) + _dft_matrices(seq, 512) + (1024, 1024)
    tabs_c = _hy_tables(lc, hy_w) + _dft_matrices(lc, lc) + (2 * lc, lc)

    ctx = ctx.reshape(1, b * lc, d)

    def mods(layer):
        m = mod[layer].reshape(8, 6, d)
        return [m[:b, k][:, None, :] for k in range(6)], [m[b:b + 1, k][:, None, :] for k in range(6)]

    m_x, m_c = mods(0)
    hx = _resnorm(x, g_norm=norm_g[0, 0], shift=m_x[0], scale=m_x[1])
    hc = _resnorm(ctx, g_norm=norm_g[0, 0], shift=m_c[0], scale=m_c[1])

    for l in range(depth):
        last = l == depth - 1
        lam_init = 0.8 - 0.6 * math.exp(-0.3 * l)
        m_x, m_c = mods(l)
        streams = [("x", x, hx, m_x, seq, b), ("c", ctx, hc, m_c, lc, b)]
        qkv_c = _mm(hc, w_in, l, 0, qkv_w, BF16, tm=b * lc, tn=512, name="mm_qkv").reshape(b, lc, qkv_w)
        qkv_x = _mm(hx, w_in, l, 0, qkv_w, BF16, tm=seq, tn=512, rope=rope, rope_cols=2 * att_w,
                    name="mm_qkv_rope")
        new = {}
        for tag, res, h, m, s_len, nb in streams:
            is_x = tag == "x"
            if last and not is_x:
                continue
            g_rows = res.shape[0] * res.shape[1]
            rest = _mm(h, w_in, l, qkv_w, rest_w, F32, tm=h.shape[1], tn=512, name="mm_rest")
            rest_seq = rest.reshape(nb, s_len, rest_w)
            if is_x:
                att = _attention(qkv_x, qkv_c, diff_lam, attn_subln_g, l, lam_init, tq=512, n_sub=4)
            else:
                att = _attention(qkv_c, None, diff_lam, attn_subln_g, l, lam_init, tq=s_len, n_sub=2)
            hyo = _hyena_branch(rest_seq, l, p, tabs_x if is_x else tabs_c)
            poo = _pool(rest_seq, 3 * hy_w, pool_w, pool_scale, l)
            merged = _merge(att.reshape(g_rows, att_w), hyo.reshape(g_rows, hy_w),
                            poo.reshape(g_rows, pool_width), rest.reshape(g_rows, rest_w), gate_off,
                            w_att_o, w_hy_o, w_pool_o, l, tm=1024, tn=512)
            merged = merged.reshape(h.shape)
            mix = _mm(merged, w_out, l, 0, d, F32, tm=h.shape[1], tn=512, name="mm_out")
            res, h2 = _resnorm(res, mix, norm_g[l, 1], m[2], norm_g[l, 2], m[3], m[4])
            act = _ffn_up(h2, w_up, ff_conv_w, l, s_len, tn=256)
            ffn = _mm(act, w_down, l, 0, d, F32, tm=1024, tn=256, name="mm_down")
            if last:
                res = _resnorm(res, ffn, norm_g[l, 3], m[5])
                h_next = None
            else:
                m_nx, m_nc = mods(l + 1)
                m_n = m_nx if is_x else m_nc
                res, h_next = _resnorm(res, ffn, norm_g[l, 3], m[5], norm_g[l + 1, 0], m_n[0], m_n[1])
            new[tag] = (res, h_next)
        x, hx = new["x"]
        if not last:
            ctx, hc = new["c"]
    return x
```

```python
import functools
import math

import jax
import jax.numpy as jnp
from jax import lax
from jax.experimental import pallas as pl
from jax.experimental.pallas import tpu as pltpu

F32 = jnp.float32
BF16 = jnp.bfloat16

GRID_W = 64
N_BRANCH = 3
ATT_HEAD_DIM = 64
ATT_V_DIM = 2 * ATT_HEAD_DIM
ROPE_BASE = 10000.0
HY_ORDER = 2
HY_BANDS = 16
HY_EMB_DIM = 1 + 2 * HY_BANDS
HY_MIN_DECAY = math.log(1e-2) / 1.5
HY_MAX_DECAY = math.log(1e-2) / 0.3
POOL_WINDOWS = (2, 4, 8, 16)
EPS = 1e-6

V7X_VMEM_BYTES = 64 * 1024 * 1024
VMEM_LIMIT = V7X_VMEM_BYTES - 8 * 1024 * 1024
LANE = 128


def _params(n_axes):
    return pltpu.CompilerParams(dimension_semantics=("arbitrary",) * n_axes,
                                vmem_limit_bytes=VMEM_LIMIT)


def _bdot(a, b):
    return jnp.dot(a.astype(BF16), b.astype(BF16), preferred_element_type=F32)


def _sigmoid(x):
    return 1.0 / (1.0 + jnp.exp(-x))


def _rms(x, g):
    return x * lax.rsqrt(jnp.mean(x * x, axis=-1, keepdims=True) + EPS) * g


def _ada_kernel(c_ref, w_ref, b_ref, o_ref):
    c = c_ref[...]
    o_ref[...] = _bdot(c * _sigmoid(c), w_ref[...]) + b_ref[...]


def _ada(cond, w_ada, b_ada):
    depth, d, n = w_ada.shape
    rows = cond.shape[0]
    tn = 1024
    return pl.pallas_call(
        _ada_kernel,
        grid=(depth, n // tn),
        in_specs=[pl.BlockSpec((rows, d), lambda l, j: (0, 0)),
                  pl.BlockSpec((None, d, tn), lambda l, j: (l, 0, j)),
                  pl.BlockSpec((None, 1, tn), lambda l, j: (l, 0, j))],
        out_specs=pl.BlockSpec((None, rows, tn), lambda l, j: (l, 0, j)),
        out_shape=jax.ShapeDtypeStruct((depth, rows, n), F32),
        compiler_params=_params(2),
        name="ada",
    )(cond, w_ada, b_ada.reshape(depth, 1, n))


def _resnorm_kernel(*refs, has_res, has_norm):
    it = iter(refs)
    x_ref = next(it)
    if has_res:
        y_ref, gres_ref, gate_ref = next(it), next(it), next(it)
    if has_norm:
        gnorm_ref, shift_ref, scale_ref = next(it), next(it), next(it)
    x = x_ref[...]
    if has_res:
        x = x + gate_ref[...] * _rms(y_ref[...], gres_ref[...])
        xo_ref = next(it)
        xo_ref[...] = x
    if has_norm:
        h_ref = next(it)
        h_ref[...] = (_rms(x, gnorm_ref[...]) * (1.0 + scale_ref[...]) + shift_ref[...]).astype(BF16)


def _resnorm(x, y=None, g_res=None, gate=None, g_norm=None, shift=None, scale=None):
    g, s, d = x.shape
    ts = 256
    has_res, has_norm = y is not None, g_norm is not None
    row = pl.BlockSpec((None, ts, d), lambda a, i: (a, i, 0))
    vec = pl.BlockSpec((1, d), lambda a, i: (0, 0))
    mod = pl.BlockSpec((None, 1, d), lambda a, i: (a, 0, 0))
    args, specs, outs, out_specs = [x], [row], [], []
    if has_res:
        args += [y, g_res.reshape(1, d), gate]
        specs += [row, vec, mod]
        outs.append(jax.ShapeDtypeStruct((g, s, d), F32))
        out_specs.append(row)
    if has_norm:
        args += [g_norm.reshape(1, d), shift, scale]
        specs += [vec, mod, mod]
        outs.append(jax.ShapeDtypeStruct((g, s, d), BF16))
        out_specs.append(row)
    res = pl.pallas_call(
        functools.partial(_resnorm_kernel, has_res=has_res, has_norm=has_norm),
        grid=(g, s // ts), in_specs=specs, out_specs=out_specs, out_shape=outs,
        compiler_params=_params(2), name="resnorm",
    )(*args)
    return res if len(res) > 1 else res[0]


def _mm_kernel(a_ref, w_ref, o_ref):
    o_ref[...] = _bdot(a_ref[...], w_ref[...]).astype(o_ref.dtype)


def _mm_rope_kernel(a_ref, w_ref, co_ref, sa_ref, sb_ref, o_ref, *, rope_tiles):
    acc = _bdot(a_ref[...], w_ref[...])
    j = pl.program_id(2)

    @pl.when(j < rope_tiles)
    def _():
        co, sa, sb = co_ref[...], sa_ref[...], sb_ref[...]
        quarter = ATT_HEAD_DIM // 4
        for c in range(acc.shape[1] // LANE):
            t = acc[:, c * LANE:(c + 1) * LANE]
            r = t * co + pltpu.roll(t, LANE - quarter, 1) * sa + pltpu.roll(t, quarter, 1) * sb
            o_ref[:, c * LANE:(c + 1) * LANE] = r.astype(o_ref.dtype)

    @pl.when(j >= rope_tiles)
    def _():
        o_ref[...] = acc.astype(o_ref.dtype)


def _mm_conv_kernel(a_ref, w_ref, cw_ref, o_ref, *, conv_tiles, seg):
    acc = _bdot(a_ref[...], w_ref[...])
    j = pl.program_id(2)

    @pl.when(j < conv_tiles)
    def _():
        o_ref[...] = _dwconv3(acc, cw_ref[...], seg).astype(o_ref.dtype)

    @pl.when(j >= conv_tiles)
    def _():
        o_ref[...] = acc.astype(o_ref.dtype)


def _mm(a, w, layer, col_off, n, out_dtype, *, tm, tn, rope=None, rope_cols=0, conv=None, name="mm"):
    g, s, k = a.shape
    assert col_off % tn == 0 and n % tn == 0 and s % tm == 0
    off = col_off // tn
    in_specs = [pl.BlockSpec((None, tm, k), lambda a_, i, j: (a_, i, 0)),
                pl.BlockSpec((None, k, tn), lambda a_, i, j: (layer, 0, off + j))]
    args = [a, w]
    if conv is not None:
        taps, seg = conv
        conv_tiles = taps.shape[2] // tn
        assert tm % seg == 0 and taps.shape[2] % tn == 0
        kern = functools.partial(_mm_conv_kernel, conv_tiles=conv_tiles, seg=seg)
        in_specs.append(pl.BlockSpec((None, 3, tn), lambda a_, i, j: (layer, 0, jnp.minimum(j, conv_tiles - 1))))
        args.append(taps)
    elif rope is None:
        kern = _mm_kernel
    else:
        kern = functools.partial(_mm_rope_kernel, rope_tiles=rope_cols // tn)
        in_specs += [pl.BlockSpec((tm, LANE), lambda a_, i, j: (i, 0))] * 3
        args += list(rope)
    return pl.pallas_call(
        kern, grid=(g, s // tm, n // tn), in_specs=in_specs,
        out_specs=pl.BlockSpec((None, tm, tn), lambda a_, i, j: (a_, i, j)),
        out_shape=jax.ShapeDtypeStruct((g, s, n), out_dtype),
        compiler_params=_params(3), name=name,
    )(*args)


def _attn_kernel(*refs, has_ctx, lam_init, s_own, s_ctx, n_sub):
    if has_ctx:
        q_ref, k_ref, v_ref, kc_ref, vc_ref, lam_ref, g_ref, o_ref, kk, vv = refs
    else:
        q_ref, k_ref, v_ref, lam_ref, g_ref, o_ref, kk, vv = refs

    @pl.when(pl.program_id(2) == 0)
    def _():
        kk[0:s_own, :] = k_ref[...]
        vv[0:s_own, :] = v_ref[...]
        if has_ctx:
            kk[s_own:s_own + s_ctx, :] = kc_ref[...]
            vv[s_own:s_own + s_ctx, :] = vc_ref[...]

    lp = lam_ref[...]
    lam = (jnp.exp(jnp.sum(lp[0:1] * lp[1:2], axis=-1, keepdims=True))
           - jnp.exp(jnp.sum(lp[2:3] * lp[3:4], axis=-1, keepdims=True)) + lam_init)

    ts = q_ref.shape[0] // n_sub
    for sb in range(n_sub):
        q = q_ref[sb * ts:(sb + 1) * ts, :] * (ATT_HEAD_DIM ** -0.5)
        lane = lax.broadcasted_iota(jnp.int32, q.shape, 1)
        zero = jnp.zeros_like(q)
        q2 = jnp.concatenate([jnp.where(lane < ATT_HEAD_DIM, q, zero),
                              jnp.where(lane >= ATT_HEAD_DIM, q, zero)], axis=0)
        s = lax.dot_general(q2, kk[...], (((1,), (1,)), ((), ())), preferred_element_type=F32)
        m = jnp.max(s, axis=-1, keepdims=True)
        e = jnp.exp(s - m)
        den = jnp.sum(e, axis=-1, keepdims=True)
        o = jnp.dot(e.astype(BF16), vv[...], preferred_element_type=F32) * (1.0 / den)
        att = o[:ts] - lam * o[ts:]
        o_ref[sb * ts:(sb + 1) * ts, :] = (_rms(att, g_ref[...]) * (1.0 - lam_init)).astype(o_ref.dtype)


def _attention(qkv, qkv_ctx, diff_lam, subln_g, layer, lam_init, *, tq, n_sub):
    b, s, w3 = qkv.shape
    hd = ATT_V_DIM
    heads = w3 // (3 * hd)
    has_ctx = qkv_ctx is not None
    s_ctx = qkv_ctx.shape[1] if has_ctx else 0
    in_specs = [pl.BlockSpec((None, tq, hd), lambda b_, h, i: (b_, i, h)),
                pl.BlockSpec((None, s, hd), lambda b_, h, i: (b_, 0, heads + h)),
                pl.BlockSpec((None, s, hd), lambda b_, h, i: (b_, 0, 2 * heads + h))]
    args = [qkv, qkv, qkv]
    if has_ctx:
        in_specs += [pl.BlockSpec((None, s_ctx, hd), lambda b_, h, i: (b_, 0, heads + h)),
                     pl.BlockSpec((None, s_ctx, hd), lambda b_, h, i: (b_, 0, 2 * heads + h))]
        args += [qkv_ctx, qkv_ctx]
    in_specs += [pl.BlockSpec((None, 4, ATT_HEAD_DIM), lambda b_, h, i: (layer, 0, 0)),
                 pl.BlockSpec((None, 1, hd), lambda b_, h, i: (layer, 0, 0))]
    args += [diff_lam, subln_g.reshape(subln_g.shape[0], 1, hd)]
    return pl.pallas_call(
        functools.partial(_attn_kernel, has_ctx=has_ctx, lam_init=lam_init, s_own=s, s_ctx=s_ctx,
                          n_sub=n_sub),
        grid=(b, heads, s // tq), in_specs=in_specs,
        out_specs=pl.BlockSpec((None, tq, hd), lambda b_, h, i: (b_, i, h)),
        out_shape=jax.ShapeDtypeStruct((b, s, heads * hd), BF16),
        scratch_shapes=[pltpu.VMEM((s + s_ctx, hd), BF16), pltpu.VMEM((s + s_ctx, hd), BF16)],
        compiler_params=_params(3), name="attn",
    )(*args)


def _shift_rows(x, d, seg):
    n = x.shape[0]
    if d == 0:
        return x
    rolled = pltpu.roll(x, (-d) % n, 0)
    pos = lax.broadcasted_iota(jnp.int32, x.shape, 0) % seg
    ok = (pos + d >= 0) & (pos + d < seg)
    return jnp.where(ok, rolled, jnp.zeros_like(x))


def _dwconv3(x, w, seg):
    return _shift_rows(x, -1, seg) * w[0:1] + x * w[1:2] + _shift_rows(x, 1, seg) * w[2:3]


def _hdot(a, b):
    return jnp.dot(a, b, preferred_element_type=F32, precision=lax.Precision.HIGHEST)


def _filter_kernel(z_ref, w1_ref, b1_ref, w2_ref, b2_ref, w3f_ref, w3b_ref, fr_ref, dec_ref, e_ref, o_ref):
    fr = fr_ref[...]
    h = jnp.sin(fr[0:1] * (_hdot(z_ref[...], w1_ref[...]) + b1_ref[...]))
    h = jnp.sin(fr[1:2] * (_hdot(h, w2_ref[...]) + b2_ref[...]))
    dec = dec_ref[...]
    fwd = _hdot(h, w3f_ref[...]) * dec
    bwd = _hdot(h, w3b_ref[...]) * dec
    row = lax.broadcasted_iota(jnp.int32, bwd.shape, 0)
    bwd = jnp.where(row == 0, jnp.zeros_like(bwd), bwd)
    norm = jnp.sum(jnp.abs(fwd), axis=0, keepdims=True) + jnp.sum(jnp.abs(bwd), axis=0, keepdims=True) + EPS
    e_ref[...] = (fwd + bwd) / norm
    o_ref[...] = (fwd - bwd) / norm


def _hy_filters(z, dec, w1, b1, w2, b2, w3, freq, layer, width):
    s = z.shape[0]
    depth, emb, fd = w1.shape
    w1p = jnp.pad(w1, ((0, 0), (0, z.shape[1] - emb), (0, 0)))
    tc = 256
    nct = width // tc
    n_oc = HY_ORDER * nct
    return pl.pallas_call(
        _filter_kernel, grid=(n_oc,),
        in_specs=[pl.BlockSpec((s, z.shape[1]), lambda j: (0, 0)),
                  pl.BlockSpec((None, z.shape[1], fd), lambda j: (layer, 0, 0)),
                  pl.BlockSpec((None, 1, fd), lambda j: (layer, 0, 0)),
                  pl.BlockSpec((None, fd, fd), lambda j: (layer, 0, 0)),
                  pl.BlockSpec((None, 1, fd), lambda j: (layer, 0, 0)),
                  pl.BlockSpec((None, fd, tc), lambda j: (layer, 0, j)),
                  pl.BlockSpec((None, fd, tc), lambda j: (layer, 0, n_oc + j)),
                  pl.BlockSpec((None, 2, fd), lambda j: (layer, 0, 0)),
                  pl.BlockSpec((s, tc), lambda j: (0, j % nct))],
        out_specs=[pl.BlockSpec((s, tc), lambda j: (0, j))] * 2,
        out_shape=[jax.ShapeDtypeStruct((s, HY_ORDER * width), F32)] * 2,
        compiler_params=_params(1), name="hy_filter",
    )(z, w1p, b1.reshape(depth, 1, fd), w2, b2.reshape(depth, 1, fd), w3, w3, freq, dec)


def _spectrum_kernel(f_ref, e_ref, o_ref, k_ref, *, n_fft):
    hb = f_ref.shape[0] // 2
    e = e_ref[...].astype(BF16)
    acc_e = _bdot(f_ref[0:hb, :], e)
    acc_o = _bdot(f_ref[hb:, :], o_ref[...])
    nyq = _bdot(f_ref[hb:hb + 16, :], e)[0:1]
    first = lax.broadcasted_iota(jnp.int32, acc_e.shape, 0) + pl.program_id(0) * hb == 0
    scale = jnp.where(first, 1.0 / n_fft, 2.0 / n_fft)
    k_ref[0:hb, :] = acc_e * scale
    k_ref[hb:, :] = jnp.where(first, nyq, acc_o) * scale


def _hy_spectrum(fmat, e, o, *, tmf):
    n_fft, s = fmat.shape
    n = e.shape[1]
    tn = 512
    return pl.pallas_call(
        functools.partial(_spectrum_kernel, n_fft=n_fft), grid=(n_fft // tmf, n // tn),
        in_specs=[pl.BlockSpec((tmf, s), lambda i, j: (i, 0)),
                  pl.BlockSpec((s, tn), lambda i, j: (0, j)),
                  pl.BlockSpec((s, tn), lambda i, j: (0, j))],
        out_specs=pl.BlockSpec((tmf, tn), lambda i, j: (i, j)),
        out_shape=jax.ShapeDtypeStruct((n_fft, n), F32),
        compiler_params=_params(2), name="hy_spectrum",
    )(fmat, e, o)


def _fwd_dft_kernel(f_ref, u_ref, k_ref, y_ref):
    acc = _bdot(f_ref[...], u_ref[...])
    hb = acc.shape[0] // 2
    ua, ub = acc[:hb], acc[hb:]
    ka, kb = k_ref[0:hb, :], k_ref[hb:, :]
    first = lax.broadcasted_iota(jnp.int32, ua.shape, 0) + pl.program_id(0) * hb == 0
    ya = ua * ka - jnp.where(first, jnp.zeros_like(ub), ub * kb)
    yb = jnp.where(first, ub * kb, ua * kb + ub * ka)
    y_ref[0:hb, :] = ya.astype(y_ref.dtype)
    y_ref[hb:, :] = yb.astype(y_ref.dtype)


def _hy_fwd(fmat, u, ucol, kf, order, *, tmf):
    n_fft, s = fmat.shape
    b = u.shape[0]
    c = kf.shape[1] // HY_ORDER
    return pl.pallas_call(
        _fwd_dft_kernel, grid=(n_fft // tmf, b),
        in_specs=[pl.BlockSpec((tmf, s), lambda i, b_: (i, 0)),
                  pl.BlockSpec((None, s, c), lambda i, b_: (b_, 0, ucol)),
                  pl.BlockSpec((tmf, c), lambda i, b_: (i, order))],
        out_specs=pl.BlockSpec((None, tmf, c), lambda i, b_: (b_, i, 0)),
        out_shape=jax.ShapeDtypeStruct((b, n_fft, c), BF16),
        compiler_params=_params(2), name="hy_fwd",
    )(fmat, u, kf)


def _inv_dft_kernel(ft_ref, y_ref, u_ref, g_ref, bias_ref, o_ref):
    conv = jnp.dot(ft_ref[...], y_ref[...], preferred_element_type=F32)
    o_ref[...] = (g_ref[...] * (conv + u_ref[...] * bias_ref[...])).astype(o_ref.dtype)


def _hy_inv(fmat_t, yf, u, ucol, gate, gcol, hy_bias, bias_row, out_dtype, *, tmi):
    s, n_fft = fmat_t.shape
    b, _, c = yf.shape
    return pl.pallas_call(
        _inv_dft_kernel, grid=(s // tmi, b),
        in_specs=[pl.BlockSpec((tmi, n_fft), lambda i, b_: (i, 0)),
                  pl.BlockSpec((None, n_fft, c), lambda i, b_: (b_, 0, 0)),
                  pl.BlockSpec((None, tmi, c), lambda i, b_: (b_, i, ucol)),
                  pl.BlockSpec((None, tmi, c), lambda i, b_: (b_, i, gcol)),
                  pl.BlockSpec((None, 1, c), lambda i, b_: (bias_row, 0, 0))],
        out_specs=pl.BlockSpec((None, tmi, c), lambda i, b_: (b_, i, 0)),
        out_shape=jax.ShapeDtypeStruct((b, s, c), out_dtype),
        compiler_params=_params(2), name="hy_inv",
    )(fmat_t, yf, u, gate, hy_bias)


def _pool_kernel(x_ref, w_ref, sc_ref, o_ref):
    x = x_ref[...]
    s = x.shape[0]
    pos = lax.broadcasted_iota(jnp.int32, x.shape, 0)
    grp = pl.program_id(1)
    for gi, win in enumerate(POOL_WINDOWS):
        @pl.when(grp == gi)
        def _(win=win):
            half = win // 2
            tot = x
            for d in range(-half, win - half):
                if d != 0:
                    tot = tot + _shift_rows(x, d, s)
            cnt = jnp.minimum(pos + (win - half), s) - jnp.maximum(pos - half, 0)
            mean = tot / cnt.astype(F32)
            o_ref[...] = (_bdot(mean - x, w_ref[...]) * sc_ref[...]).astype(o_ref.dtype)


def _pool(rest, col_off, pool_w, pool_scale, layer):
    b, s, _ = rest.shape
    depth, groups, gd, _ = pool_w.shape
    off = col_off // gd
    return pl.pallas_call(
        _pool_kernel, grid=(b, groups),
        in_specs=[pl.BlockSpec((None, s, gd), lambda b_, g: (b_, 0, off + g)),
                  pl.BlockSpec((None, None, gd, gd), lambda b_, g: (layer, g, 0, 0)),
                  pl.BlockSpec((None, 1, gd), lambda b_, g: (layer, 0, g))],
        out_specs=pl.BlockSpec((None, s, gd), lambda b_, g: (b_, 0, g)),
        out_shape=jax.ShapeDtypeStruct((b, s, groups * gd), BF16),
        compiler_params=_params(2), name="pool",
    )(rest, pool_w, pool_scale.reshape(depth, 1, groups * gd))


def _merge_kernel(att_ref, hy_ref, po_ref, wa_ref, wh_ref, wp_ref, ga_ref, gh_ref, gp_ref, o_ref):
    acc = _sigmoid(ga_ref[...]) * _bdot(att_ref[...], wa_ref[...])
    acc = acc + _sigmoid(gh_ref[...]) * _bdot(hy_ref[...], wh_ref[...])
    acc = acc + _sigmoid(gp_ref[...]) * _bdot(po_ref[...], wp_ref[...])
    o_ref[...] = acc.astype(o_ref.dtype)


def _merge(att, hyo, poo, rest, gate_off, w_att_o, w_hy_o, w_pool_o, layer, *, tm, tn):
    r = att.shape[0]
    d = w_att_o.shape[2]
    goff = gate_off // tn
    nd = d // tn

    def a_spec(width):
        return pl.BlockSpec((tm, width), lambda i, j: (i, 0))

    def w_spec(width):
        return pl.BlockSpec((None, width, tn), lambda i, j: (layer, 0, j))

    def g_spec(branch):
        return pl.BlockSpec((tm, tn), lambda i, j: (i, goff + branch * nd + j))

    return pl.pallas_call(
        _merge_kernel, grid=(r // tm, nd),
        in_specs=[a_spec(att.shape[1]), a_spec(hyo.shape[1]), a_spec(poo.shape[1]),
                  w_spec(att.shape[1]), w_spec(hyo.shape[1]), w_spec(poo.shape[1]),
                  g_spec(0), g_spec(1), g_spec(2)],
        out_specs=pl.BlockSpec((tm, tn), lambda i, j: (i, j)),
        out_shape=jax.ShapeDtypeStruct((r, d), BF16),
        compiler_params=_params(2), name="merge",
    )(att, hyo, poo, w_att_o, w_hy_o, w_pool_o, rest, rest, rest)


def _gelu_tanh(x):
    return 0.5 * x * (1.0 + jnp.tanh(math.sqrt(2.0 / math.pi) * (x + 0.044715 * (x * x * x))))


def _ffn_up_kernel(a_ref, wg_ref, wv_ref, cw_ref, o_ref, *, seg):
    a = a_ref[...]
    gate = _dwconv3(_bdot(a, wg_ref[...]), cw_ref[...], seg)
    o_ref[...] = (_gelu_tanh(gate) * _bdot(a, wv_ref[...])).astype(o_ref.dtype)


def _ffn_up(h, w_up, conv_w, layer, seg, *, tn):
    g, s, k = h.shape
    ff = w_up.shape[2] // 2
    nt = ff // tn
    return pl.pallas_call(
        functools.partial(_ffn_up_kernel, seg=seg), grid=(g, nt),
        in_specs=[pl.BlockSpec((None, s, k), lambda a_, j: (a_, 0, 0)),
                  pl.BlockSpec((None, k, tn), lambda a_, j: (layer, 0, j)),
                  pl.BlockSpec((None, k, tn), lambda a_, j: (layer, 0, nt + j)),
                  pl.BlockSpec((None, 3, tn), lambda a_, j: (layer, 0, j))],
        out_specs=pl.BlockSpec((None, s, tn), lambda a_, j: (a_, 0, j)),
        out_shape=jax.ShapeDtypeStruct((g, s, ff), BF16),
        compiler_params=_params(2), name="ffn_up",
    )(h, w_up, w_up, conv_w)


def _rope_tables(seq):
    rows = seq // GRID_W
    row = jnp.repeat(jnp.arange(rows, dtype=F32), GRID_W)
    col = jnp.tile(jnp.arange(GRID_W, dtype=F32), rows)
    n_freq = ATT_HEAD_DIM // 4
    inv = ROPE_BASE ** (-jnp.arange(n_freq, dtype=F32) / n_freq)
    ang = jnp.stack([row[:, None] * inv, col[:, None] * inv], axis=1)
    cos, sin = jnp.cos(ang), jnp.sin(ang)
    zero = jnp.zeros_like(sin)
    co = jnp.stack([cos, cos], axis=2).reshape(seq, ATT_HEAD_DIM)
    sa = jnp.stack([-sin, zero], axis=2).reshape(seq, ATT_HEAD_DIM)
    sb = jnp.stack([zero, sin], axis=2).reshape(seq, ATT_HEAD_DIM)
    rep = LANE // ATT_HEAD_DIM
    return tuple(jnp.tile(t, (1, rep)) for t in (co, sa, sb))


def _dft_matrices(s, hb):
    n_fft = 2 * s
    lo = min(s, 64)
    hi = s // lo
    sp = jnp.arange(s, dtype=jnp.int32)[None, :]
    f_lo = jnp.arange(lo, dtype=jnp.int32)[:, None]
    f_hi = jnp.arange(hi, dtype=jnp.int32)[:, None] * lo
    ang_lo = (2.0 * math.pi / n_fft) * ((f_lo * sp) % n_fft).astype(F32)
    ang_hi = (2.0 * math.pi / n_fft) * ((f_hi * sp) % n_fft).astype(F32)
    c1, s1 = jnp.cos(ang_lo)[None], jnp.sin(ang_lo)[None]
    c2, s2 = jnp.cos(ang_hi)[:, None], jnp.sin(ang_hi)[:, None]
    cmat = (c1 * c2 - s1 * s2).reshape(s, s)
    smat = (s1 * c2 + c1 * s2).reshape(s, s)
    nyq = jnp.where(jnp.arange(s) % 2 == 0, 1.0, -1.0).astype(F32)
    smat = smat.at[0].set(nyq)
    fmat = jnp.concatenate([cmat.reshape(s // hb, hb, s), smat.reshape(s // hb, hb, s)], axis=1)
    fmat = fmat.reshape(n_fft, s).astype(BF16)
    return fmat, fmat.T


def _hy_tables(s, width):
    t = jnp.linspace(0.0, 1.0, s, dtype=F32)[:, None]
    w_ang = 2.0 * math.pi * jnp.arange(s, dtype=F32)[:, None] / s
    f = jnp.linspace(1e-4, HY_BANDS - 1, HY_BANDS, dtype=F32)[None, :]
    z = jnp.concatenate([t, jnp.cos(f * w_ang), -jnp.sin(f * w_ang)], axis=-1)
    z = jnp.pad(z, ((0, 0), (0, LANE - HY_EMB_DIM)))
    deltas = jnp.abs(jnp.linspace(HY_MIN_DECAY, HY_MAX_DECAY, width, dtype=F32))
    return z, jnp.exp(-t * deltas[None, :])


def _hyena_branch(conv, layer, p, tabs):
    width = p["w_hy_o"].shape[1]
    z, dec, fmat, fmat_t, tmf, tmi = tabs
    e, o = _hy_filters(z, dec, p["hy_ffn_w1"], p["hy_ffn_b1"], p["hy_ffn_w2"], p["hy_ffn_b2"],
                       p["hy_ffn_w3"], p["hy_freq"], layer, width)
    kf = _hy_spectrum(fmat, e, o, tmf=tmf)
    depth = p["hy_bias"].shape[0]
    bias = p["hy_bias"].reshape(depth * HY_ORDER, 1, width)
    yf = _hy_fwd(fmat, conv, 0, kf, 0, tmf=tmf)
    zsig = _hy_inv(fmat_t, yf, conv, 0, conv, 1, bias, layer * HY_ORDER, F32, tmi=tmi)
    yf = _hy_fwd(fmat, zsig, 0, kf, 1, tmf=tmf)
    return _hy_inv(fmat_t, yf, zsig, 0, conv, 2, bias, layer * HY_ORDER + 1, BF16, tmi=tmi)


def kernel(x, c, ctx, c_ctx, w_ada, b_ada, norm_g, w_in, diff_lam, attn_subln_g, hy_short_w,
           hy_ffn_w1, hy_ffn_b1, hy_ffn_w2, hy_ffn_b2, hy_ffn_w3, hy_freq, hy_bias, pool_w, pool_scale,
           w_att_o, w_hy_o, w_pool_o, w_out, w_up, ff_conv_w, w_down):
    p = dict(hy_short_w=hy_short_w, hy_ffn_w1=hy_ffn_w1, hy_ffn_b1=hy_ffn_b1, hy_ffn_w2=hy_ffn_w2,
             hy_ffn_b2=hy_ffn_b2, hy_ffn_w3=hy_ffn_w3, hy_freq=hy_freq, hy_bias=hy_bias, w_hy_o=w_hy_o)
    b, seq, d = x.shape
    lc = ctx.shape[1]
    depth = w_ada.shape[0]
    att_w = w_att_o.shape[1]
    hy_w = w_hy_o.shape[1]
    pool_width = w_pool_o.shape[1]
    qkv_w = 3 * att_w
    rest_w = w_in.shape[2] - qkv_w
    gate_off = 3 * hy_w + pool_width

    cond = jnp.concatenate([c, c_ctx[None], jnp.zeros((8 - b - 1, d), F32)], axis=0)
    mod = _ada(cond, w_ada, b_ada)

    rope = _rope_tables(seq)
    tabs_x = _hy_tables(seq, hy_w) + _dft_matrices(seq, 512) + (1024, 1024)
    tabs_c = _hy_tables(lc, hy_w) + _dft_matrices(lc, lc) + (2 * lc, lc)

    ctx = ctx.reshape(1, b * lc, d)

    def mods(layer):
        m = mod[layer].reshape(8, 6, d)
        return [m[:b, k][:, None, :] for k in range(6)], [m[b:b + 1, k][:, None, :] for k in range(6)]

    m_x, m_c = mods(0)
    hx = _resnorm(x, g_norm=norm_g[0, 0], shift=m_x[0], scale=m_x[1])
    hc = _resnorm(ctx, g_norm=norm_g[0, 0], shift=m_c[0], scale=m_c[1])

    for l in range(depth):
        last = l == depth - 1
        lam_init = 0.8 - 0.6 * math.exp(-0.3 * l)
        m_x, m_c = mods(l)
        streams = [("x", x, hx, m_x, seq, b), ("c", ctx, hc, m_c, lc, b)]
        qkv_c = _mm(hc, w_in, l, 0, qkv_w, BF16, tm=b * lc, tn=512, name="mm_qkv").reshape(b, lc, qkv_w)
        qkv_x = _mm(hx, w_in, l, 0, qkv_w, BF16, tm=seq, tn=512, rope=rope, rope_cols=2 * att_w,
                    name="mm_qkv_rope")
        new = {}
        for tag, res, h, m, s_len, nb in streams:
            is_x = tag == "x"
            if last and not is_x:
                continue
            g_rows = res.shape[0] * res.shape[1]
            rest = _mm(h, w_in, l, qkv_w, rest_w, F32, tm=h.shape[1], tn=512, conv=(hy_short_w, s_len),
                       name="mm_rest")
            rest_seq = rest.reshape(nb, s_len, rest_w)
            if is_x:
                att = _attention(qkv_x, qkv_c, diff_lam, attn_subln_g, l, lam_init, tq=1024, n_sub=4)
            else:
                att = _attention(qkv_c, None, diff_lam, attn_subln_g, l, lam_init, tq=s_len, n_sub=2)
            hyo = _hyena_branch(rest_seq, l, p, tabs_x if is_x else tabs_c)
            poo = _pool(rest_seq, 3 * hy_w, pool_w, pool_scale, l)
            merged = _merge(att.reshape(g_rows, att_w), hyo.reshape(g_rows, hy_w),
                            poo.reshape(g_rows, pool_width), rest.reshape(g_rows, rest_w), gate_off,
                            w_att_o, w_hy_o, w_pool_o, l, tm=min(g_rows, 2048), tn=256)
            merged = merged.reshape(h.shape)
            mix = _mm(merged, w_out, l, 0, d, F32, tm=h.shape[1], tn=512, name="mm_out")
            res, h2 = _resnorm(res, mix, norm_g[l, 1], m[2], norm_g[l, 2], m[3], m[4])
            act = _ffn_up(h2, w_up, ff_conv_w, l, s_len, tn=256)
            ffn = _mm(act, w_down, l, 0, d, F32, tm=1024, tn=256, name="mm_down")
            if last:
                res = _resnorm(res, ffn, norm_g[l, 3], m[5])
                h_next = None
            else:
                m_nx, m_nc = mods(l + 1)
                m_n = m_nx if is_x else m_nc
                res, h_next = _resnorm(res, ffn, norm_g[l, 3], m[5], norm_g[l + 1, 0], m_n[0], m_n[1])
            new[tag] = (res, h_next)
        x, hx = new["x"]
        if not last:
            ctx, hc = new["c"]
    return x
```

```python
import functools
import math

import jax
import jax.numpy as jnp
from jax import lax
from jax.experimental import pallas as pl
from jax.experimental.pallas import tpu as pltpu

F32 = jnp.float32
BF16 = jnp.bfloat16

GRID_W = 64
N_BRANCH = 3
ATT_HEAD_DIM = 64
ATT_V_DIM = 2 * ATT_HEAD_DIM
ROPE_BASE = 10000.0
HY_ORDER = 2
HY_BANDS = 16
HY_EMB_DIM = 1 + 2 * HY_BANDS
HY_MIN_DECAY = math.log(1e-2) / 1.5
HY_MAX_DECAY = math.log(1e-2) / 0.3
POOL_WINDOWS = (2, 4, 8, 16)
EPS = 1e-6

V7X_VMEM_BYTES = 64 * 1024 * 1024
VMEM_LIMIT = V7X_VMEM_BYTES - 8 * 1024 * 1024
LANE = 128


def _params(n_axes):
    return pltpu.CompilerParams(dimension_semantics=("arbitrary",) * n_axes,
                                vmem_limit_bytes=VMEM_LIMIT)


def _bdot(a, b):
    return jnp.dot(a.astype(BF16), b.astype(BF16), preferred_element_type=F32)


def _sigmoid(x):
    return 1.0 / (1.0 + jnp.exp(-x))


def _rms(x, g):
    return x * lax.rsqrt(jnp.mean(x * x, axis=-1, keepdims=True) + EPS) * g


def _ada_kernel(c_ref, w_ref, b_ref, o_ref):
    c = c_ref[...]
    o_ref[...] = _bdot(c * _sigmoid(c), w_ref[...]) + b_ref[...]


def _ada(cond, w_ada, b_ada):
    depth, d, n = w_ada.shape
    rows = cond.shape[0]
    tn = 1024
    return pl.pallas_call(
        _ada_kernel,
        grid=(depth, n // tn),
        in_specs=[pl.BlockSpec((rows, d), lambda l, j: (0, 0)),
                  pl.BlockSpec((None, d, tn), lambda l, j: (l, 0, j)),
                  pl.BlockSpec((None, 1, tn), lambda l, j: (l, 0, j))],
        out_specs=pl.BlockSpec((None, rows, tn), lambda l, j: (l, 0, j)),
        out_shape=jax.ShapeDtypeStruct((depth, rows, n), F32),
        compiler_params=_params(2),
        name="ada",
    )(cond, w_ada, b_ada.reshape(depth, 1, n))


def _resnorm_kernel(*refs, has_res, has_norm):
    it = iter(refs)
    x_ref = next(it)
    if has_res:
        y_ref, gres_ref, gate_ref = next(it), next(it), next(it)
    if has_norm:
        gnorm_ref, shift_ref, scale_ref = next(it), next(it), next(it)
    x = x_ref[...]
    if has_res:
        x = x + gate_ref[...] * _rms(y_ref[...], gres_ref[...])
        xo_ref = next(it)
        xo_ref[...] = x
    if has_norm:
        h_ref = next(it)
        h_ref[...] = (_rms(x, gnorm_ref[...]) * (1.0 + scale_ref[...]) + shift_ref[...]).astype(BF16)


def _resnorm(x, y=None, g_res=None, gate=None, g_norm=None, shift=None, scale=None):
    g, s, d = x.shape
    ts = 256
    has_res, has_norm = y is not None, g_norm is not None
    row = pl.BlockSpec((None, ts, d), lambda a, i: (a, i, 0))
    vec = pl.BlockSpec((1, d), lambda a, i: (0, 0))
    mod = pl.BlockSpec((None, 1, d), lambda a, i: (a, 0, 0))
    args, specs, outs, out_specs = [x], [row], [], []
    if has_res:
        args += [y, g_res.reshape(1, d), gate]
        specs += [row, vec, mod]
        outs.append(jax.ShapeDtypeStruct((g, s, d), F32))
        out_specs.append(row)
    if has_norm:
        args += [g_norm.reshape(1, d), shift, scale]
        specs += [vec, mod, mod]
        outs.append(jax.ShapeDtypeStruct((g, s, d), BF16))
        out_specs.append(row)
    res = pl.pallas_call(
        functools.partial(_resnorm_kernel, has_res=has_res, has_norm=has_norm),
        grid=(g, s // ts), in_specs=specs, out_specs=out_specs, out_shape=outs,
        compiler_params=_params(2), name="resnorm",
    )(*args)
    return res if len(res) > 1 else res[0]


def _mm_kernel(a_ref, w_ref, o_ref):
    o_ref[...] = _bdot(a_ref[...], w_ref[...]).astype(o_ref.dtype)


def _mm_rope_kernel(a_ref, w_ref, co_ref, sa_ref, sb_ref, o_ref, *, rope_tiles):
    acc = _bdot(a_ref[...], w_ref[...])
    j = pl.program_id(2)

    @pl.when(j < rope_tiles)
    def _():
        co, sa, sb = co_ref[...], sa_ref[...], sb_ref[...]
        quarter = ATT_HEAD_DIM // 4
        for c in range(acc.shape[1] // LANE):
            t = acc[:, c * LANE:(c + 1) * LANE]
            r = t * co + pltpu.roll(t, LANE - quarter, 1) * sa + pltpu.roll(t, quarter, 1) * sb
            o_ref[:, c * LANE:(c + 1) * LANE] = r.astype(o_ref.dtype)

    @pl.when(j >= rope_tiles)
    def _():
        o_ref[...] = acc.astype(o_ref.dtype)


def _mm_conv_kernel(a_ref, w_ref, cw_ref, o_ref, *, conv_tiles, seg):
    acc = _bdot(a_ref[...], w_ref[...])
    j = pl.program_id(2)

    @pl.when(j < conv_tiles)
    def _():
        o_ref[...] = _dwconv3(acc, cw_ref[...], seg).astype(o_ref.dtype)

    @pl.when(j >= conv_tiles)
    def _():
        o_ref[...] = acc.astype(o_ref.dtype)


def _mm(a, w, layer, col_off, n, out_dtype, *, tm, tn, rope=None, rope_cols=0, conv=None, name="mm"):
    g, s, k = a.shape
    assert col_off % tn == 0 and n % tn == 0 and s % tm == 0
    off = col_off // tn
    in_specs = [pl.BlockSpec((None, tm, k), lambda a_, i, j: (a_, i, 0)),
                pl.BlockSpec((None, k, tn), lambda a_, i, j: (layer, 0, off + j))]
    args = [a, w]
    if conv is not None:
        taps, seg = conv
        conv_tiles = taps.shape[2] // tn
        assert tm % seg == 0 and taps.shape[2] % tn == 0
        kern = functools.partial(_mm_conv_kernel, conv_tiles=conv_tiles, seg=seg)
        in_specs.append(pl.BlockSpec((None, 3, tn), lambda a_, i, j: (layer, 0, jnp.minimum(j, conv_tiles - 1))))
        args.append(taps)
    elif rope is None:
        kern = _mm_kernel
    else:
        kern = functools.partial(_mm_rope_kernel, rope_tiles=rope_cols // tn)
        in_specs += [pl.BlockSpec((tm, LANE), lambda a_, i, j: (i, 0))] * 3
        args += list(rope)
    return pl.pallas_call(
        kern, grid=(g, s // tm, n // tn), in_specs=in_specs,
        out_specs=pl.BlockSpec((None, tm, tn), lambda a_, i, j: (a_, i, j)),
        out_shape=jax.ShapeDtypeStruct((g, s, n), out_dtype),
        compiler_params=_params(3), name=name,
    )(*args)


def _mm_tail_kernel(*refs, nj, has_norm):
    if has_norm:
        a_ref, w_ref, x_ref, gres_ref, gate_ref, gnorm_ref, shift_ref, scale_ref, xo_ref, h_ref, mix = refs
    else:
        a_ref, w_ref, x_ref, gres_ref, gate_ref, xo_ref, mix = refs
    j = pl.program_id(2)
    mix[j] = _bdot(a_ref[...], w_ref[...])

    @pl.when(j == nj - 1)
    def _():
        tn = mix.shape[2]
        d = nj * tn
        cols = [slice(t * tn, (t + 1) * tn) for t in range(nj)]
        ssq = sum(jnp.sum(mix[t] * mix[t], axis=-1, keepdims=True) for t in range(nj))
        inv = lax.rsqrt(ssq / d + EPS)
        for t in range(nj):
            xo_ref[:, cols[t]] = x_ref[:, cols[t]] + gate_ref[:, cols[t]] * (mix[t] * inv * gres_ref[:, cols[t]])
        if has_norm:
            ssq = sum(jnp.sum(xo_ref[:, cols[t]] * xo_ref[:, cols[t]], axis=-1, keepdims=True) for t in range(nj))
            inv = lax.rsqrt(ssq / d + EPS)
            for t in range(nj):
                hn = xo_ref[:, cols[t]] * inv * gnorm_ref[:, cols[t]]
                h_ref[:, cols[t]] = (hn * (1.0 + scale_ref[:, cols[t]]) + shift_ref[:, cols[t]]).astype(BF16)


def _mm_tail(a, w, layer, x, g_res, gate, g_norm=None, shift=None, scale=None, *, tm, tn, name):
    g, s, k = a.shape
    d = w.shape[2]
    nj = d // tn
    has_norm = g_norm is not None
    row = pl.BlockSpec((None, tm, d), lambda a_, i, j: (a_, i, 0))
    vec = pl.BlockSpec((1, d), lambda a_, i, j: (0, 0))
    mod = pl.BlockSpec((None, 1, d), lambda a_, i, j: (a_, 0, 0))
    in_specs = [pl.BlockSpec((None, tm, k), lambda a_, i, j: (a_, i, 0)),
                pl.BlockSpec((None, k, tn), lambda a_, i, j: (layer, 0, j)), row, vec, mod]
    args = [a, w, x, g_res.reshape(1, d), gate]
    outs, out_specs = [jax.ShapeDtypeStruct((g, s, d), F32)], [row]
    if has_norm:
        in_specs += [vec, mod, mod]
        args += [g_norm.reshape(1, d), shift, scale]
        outs.append(jax.ShapeDtypeStruct((g, s, d), BF16))
        out_specs.append(row)
    res = pl.pallas_call(
        functools.partial(_mm_tail_kernel, nj=nj, has_norm=has_norm),
        grid=(g, s // tm, nj), in_specs=in_specs, out_specs=out_specs, out_shape=outs,
        scratch_shapes=[pltpu.VMEM((nj, tm, tn), F32)],
        compiler_params=_params(3), name=name,
    )(*args)
    return res if has_norm else res[0]


def _attn_kernel(*refs, has_ctx, lam_init, s_own, s_ctx, n_sub):
    if has_ctx:
        q_ref, k_ref, v_ref, kc_ref, vc_ref, lam_ref, g_ref, o_ref, kk, vv = refs
    else:
        q_ref, k_ref, v_ref, lam_ref, g_ref, o_ref, kk, vv = refs

    @pl.when(pl.program_id(2) == 0)
    def _():
        kk[0:s_own, :] = k_ref[...]
        vv[0:s_own, :] = v_ref[...]
        if has_ctx:
            kk[s_own:s_own + s_ctx, :] = kc_ref[...]
            vv[s_own:s_own + s_ctx, :] = vc_ref[...]

    lp = lam_ref[...]
    lam = (jnp.exp(jnp.sum(lp[0:1] * lp[1:2], axis=-1, keepdims=True))
           - jnp.exp(jnp.sum(lp[2:3] * lp[3:4], axis=-1, keepdims=True)) + lam_init)

    ts = q_ref.shape[0] // n_sub
    for sb in range(n_sub):
        q = q_ref[sb * ts:(sb + 1) * ts, :] * (ATT_HEAD_DIM ** -0.5)
        lane = lax.broadcasted_iota(jnp.int32, q.shape, 1)
        zero = jnp.zeros_like(q)
        q2 = jnp.concatenate([jnp.where(lane < ATT_HEAD_DIM, q, zero),
                              jnp.where(lane >= ATT_HEAD_DIM, q, zero)], axis=0)
        s = lax.dot_general(q2, kk[...], (((1,), (1,)), ((), ())), preferred_element_type=F32)
        m = jnp.max(s, axis=-1, keepdims=True)
        e = jnp.exp(s - m)
        den = jnp.sum(e, axis=-1, keepdims=True)
        o = jnp.dot(e.astype(BF16), vv[...], preferred_element_type=F32) * (1.0 / den)
        att = o[:ts] - lam * o[ts:]
        o_ref[sb * ts:(sb + 1) * ts, :] = (_rms(att, g_ref[...]) * (1.0 - lam_init)).astype(o_ref.dtype)


def _attention(qkv, qkv_ctx, diff_lam, subln_g, layer, lam_init, *, tq, n_sub):
    b, s, w3 = qkv.shape
    hd = ATT_V_DIM
    heads = w3 // (3 * hd)
    has_ctx = qkv_ctx is not None
    s_ctx = qkv_ctx.shape[1] if has_ctx else 0
    in_specs = [pl.BlockSpec((None, tq, hd), lambda b_, h, i: (b_, i, h)),
                pl.BlockSpec((None, s, hd), lambda b_, h, i: (b_, 0, heads + h)),
                pl.BlockSpec((None, s, hd), lambda b_, h, i: (b_, 0, 2 * heads + h))]
    args = [qkv, qkv, qkv]
    if has_ctx:
        in_specs += [pl.BlockSpec((None, s_ctx, hd), lambda b_, h, i: (b_, 0, heads + h)),
                     pl.BlockSpec((None, s_ctx, hd), lambda b_, h, i: (b_, 0, 2 * heads + h))]
        args += [qkv_ctx, qkv_ctx]
    in_specs += [pl.BlockSpec((None, 4, ATT_HEAD_DIM), lambda b_, h, i: (layer, 0, 0)),
                 pl.BlockSpec((None, 1, hd), lambda b_, h, i: (layer, 0, 0))]
    args += [diff_lam, subln_g.reshape(subln_g.shape[0], 1, hd)]
    return pl.pallas_call(
        functools.partial(_attn_kernel, has_ctx=has_ctx, lam_init=lam_init, s_own=s, s_ctx=s_ctx,
                          n_sub=n_sub),
        grid=(b, heads, s // tq), in_specs=in_specs,
        out_specs=pl.BlockSpec((None, tq, hd), lambda b_, h, i: (b_, i, h)),
        out_shape=jax.ShapeDtypeStruct((b, s, heads * hd), BF16),
        scratch_shapes=[pltpu.VMEM((s + s_ctx, hd), BF16), pltpu.VMEM((s + s_ctx, hd), BF16)],
        compiler_params=_params(3), name="attn",
    )(*args)


def _shift_rows(x, d, seg):
    n = x.shape[0]
    if d == 0:
        return x
    rolled = pltpu.roll(x, (-d) % n, 0)
    pos = lax.broadcasted_iota(jnp.int32, x.shape, 0) % seg
    ok = (pos + d >= 0) & (pos + d < seg)
    return jnp.where(ok, rolled, jnp.zeros_like(x))


def _dwconv3(x, w, seg):
    return _shift_rows(x, -1, seg) * w[0:1] + x * w[1:2] + _shift_rows(x, 1, seg) * w[2:3]


def _hdot(a, b):
    return jnp.dot(a, b, preferred_element_type=F32, precision=lax.Precision.HIGHEST)


def _filter_kernel(z_ref, w1_ref, b1_ref, w2_ref, b2_ref, w3f_ref, w3b_ref, fr_ref, dec_ref, e_ref, o_ref):
    fr = fr_ref[...]
    h = jnp.sin(fr[0:1] * (_hdot(z_ref[...], w1_ref[...]) + b1_ref[...]))
    h = jnp.sin(fr[1:2] * (_hdot(h, w2_ref[...]) + b2_ref[...]))
    dec = dec_ref[...]
    fwd = _hdot(h, w3f_ref[...]) * dec
    bwd = _hdot(h, w3b_ref[...]) * dec
    row = lax.broadcasted_iota(jnp.int32, bwd.shape, 0)
    bwd = jnp.where(row == 0, jnp.zeros_like(bwd), bwd)
    norm = jnp.sum(jnp.abs(fwd), axis=0, keepdims=True) + jnp.sum(jnp.abs(bwd), axis=0, keepdims=True) + EPS
    e_ref[...] = (fwd + bwd) / norm
    o_ref[...] = (fwd - bwd) / norm


def _hy_filters(z, dec, w1, b1, w2, b2, w3, freq, layer, width):
    s = z.shape[0]
    depth, emb, fd = w1.shape
    w1p = jnp.pad(w1, ((0, 0), (0, z.shape[1] - emb), (0, 0)))
    tc = 256
    nct = width // tc
    n_oc = HY_ORDER * nct
    return pl.pallas_call(
        _filter_kernel, grid=(n_oc,),
        in_specs=[pl.BlockSpec((s, z.shape[1]), lambda j: (0, 0)),
                  pl.BlockSpec((None, z.shape[1], fd), lambda j: (layer, 0, 0)),
                  pl.BlockSpec((None, 1, fd), lambda j: (layer, 0, 0)),
                  pl.BlockSpec((None, fd, fd), lambda j: (layer, 0, 0)),
                  pl.BlockSpec((None, 1, fd), lambda j: (layer, 0, 0)),
                  pl.BlockSpec((None, fd, tc), lambda j: (layer, 0, j)),
                  pl.BlockSpec((None, fd, tc), lambda j: (layer, 0, n_oc + j)),
                  pl.BlockSpec((None, 2, fd), lambda j: (layer, 0, 0)),
                  pl.BlockSpec((s, tc), lambda j: (0, j % nct))],
        out_specs=[pl.BlockSpec((s, tc), lambda j: (0, j))] * 2,
        out_shape=[jax.ShapeDtypeStruct((s, HY_ORDER * width), F32)] * 2,
        compiler_params=_params(1), name="hy_filter",
    )(z, w1p, b1.reshape(depth, 1, fd), w2, b2.reshape(depth, 1, fd), w3, w3, freq, dec)


def _spectrum_kernel(f_ref, e_ref, o_ref, k_ref, *, n_fft):
    hb = f_ref.shape[0] // 2
    e = e_ref[...].astype(BF16)
    acc_e = _bdot(f_ref[0:hb, :], e)
    acc_o = _bdot(f_ref[hb:, :], o_ref[...])
    nyq = _bdot(f_ref[hb:hb + 16, :], e)[0:1]
    first = lax.broadcasted_iota(jnp.int32, acc_e.shape, 0) + pl.program_id(0) * hb == 0
    scale = jnp.where(first, 1.0 / n_fft, 2.0 / n_fft)
    k_ref[0:hb, :] = acc_e * scale
    k_ref[hb:, :] = jnp.where(first, nyq, acc_o) * scale


def _hy_spectrum(fmat, e, o, *, tmf):
    n_fft, s = fmat.shape
    n = e.shape[1]
    tn = 512
    return pl.pallas_call(
        functools.partial(_spectrum_kernel, n_fft=n_fft), grid=(n_fft // tmf, n // tn),
        in_specs=[pl.BlockSpec((tmf, s), lambda i, j: (i, 0)),
                  pl.BlockSpec((s, tn), lambda i, j: (0, j)),
                  pl.BlockSpec((s, tn), lambda i, j: (0, j))],
        out_specs=pl.BlockSpec((tmf, tn), lambda i, j: (i, j)),
        out_shape=jax.ShapeDtypeStruct((n_fft, n), F32),
        compiler_params=_params(2), name="hy_spectrum",
    )(fmat, e, o)


def _fwd_dft_kernel(f_ref, u_ref, k_ref, y_ref):
    acc = _bdot(f_ref[...], u_ref[...])
    hb = acc.shape[0] // 2
    ua, ub = acc[:hb], acc[hb:]
    ka, kb = k_ref[0:hb, :], k_ref[hb:, :]
    first = lax.broadcasted_iota(jnp.int32, ua.shape, 0) + pl.program_id(0) * hb == 0
    ya = ua * ka - jnp.where(first, jnp.zeros_like(ub), ub * kb)
    yb = jnp.where(first, ub * kb, ua * kb + ub * ka)
    y_ref[0:hb, :] = ya.astype(y_ref.dtype)
    y_ref[hb:, :] = yb.astype(y_ref.dtype)


def _hy_fwd(fmat, u, ucol, kf, order, *, tmf):
    n_fft, s = fmat.shape
    b = u.shape[0]
    c = kf.shape[1] // HY_ORDER
    return pl.pallas_call(
        _fwd_dft_kernel, grid=(n_fft // tmf, b),
        in_specs=[pl.BlockSpec((tmf, s), lambda i, b_: (i, 0)),
                  pl.BlockSpec((None, s, c), lambda i, b_: (b_, 0, ucol)),
                  pl.BlockSpec((tmf, c), lambda i, b_: (i, order))],
        out_specs=pl.BlockSpec((None, tmf, c), lambda i, b_: (b_, i, 0)),
        out_shape=jax.ShapeDtypeStruct((b, n_fft, c), BF16),
        compiler_params=_params(2), name="hy_fwd",
    )(fmat, u, kf)


def _inv_dft_kernel(ft_ref, y_ref, u_ref, g_ref, bias_ref, o_ref):
    conv = jnp.dot(ft_ref[...], y_ref[...], preferred_element_type=F32)
    o_ref[...] = (g_ref[...] * (conv + u_ref[...] * bias_ref[...])).astype(o_ref.dtype)


def _hy_inv(fmat_t, yf, u, ucol, gate, gcol, hy_bias, bias_row, out_dtype, *, tmi):
    s, n_fft = fmat_t.shape
    b, _, c = yf.shape
    return pl.pallas_call(
        _inv_dft_kernel, grid=(s // tmi, b),
        in_specs=[pl.BlockSpec((tmi, n_fft), lambda i, b_: (i, 0)),
                  pl.BlockSpec((None, n_fft, c), lambda i, b_: (b_, 0, 0)),
                  pl.BlockSpec((None, tmi, c), lambda i, b_: (b_, i, ucol)),
                  pl.BlockSpec((None, tmi, c), lambda i, b_: (b_, i, gcol)),
                  pl.BlockSpec((None, 1, c), lambda i, b_: (bias_row, 0, 0))],
        out_specs=pl.BlockSpec((None, tmi, c), lambda i, b_: (b_, i, 0)),
        out_shape=jax.ShapeDtypeStruct((b, s, c), out_dtype),
        compiler_params=_params(2), name="hy_inv",
    )(fmat_t, yf, u, gate, hy_bias)


def _pool_kernel(x_ref, w_ref, sc_ref, o_ref):
    x = x_ref[...]
    s = x.shape[0]
    pos = lax.broadcasted_iota(jnp.int32, x.shape, 0)
    grp = pl.program_id(1)
    for gi, win in enumerate(POOL_WINDOWS):
        @pl.when(grp == gi)
        def _(win=win):
            half = win // 2
            tot = x
            for d in range(-half, win - half):
                if d != 0:
                    tot = tot + _shift_rows(x, d, s)
            cnt = jnp.minimum(pos + (win - half), s) - jnp.maximum(pos - half, 0)
            mean = tot / cnt.astype(F32)
            o_ref[...] = (_bdot(mean - x, w_ref[...]) * sc_ref[...]).astype(o_ref.dtype)


def _pool(rest, col_off, pool_w, pool_scale, layer):
    b, s, _ = rest.shape
    depth, groups, gd, _ = pool_w.shape
    off = col_off // gd
    return pl.pallas_call(
        _pool_kernel, grid=(b, groups),
        in_specs=[pl.BlockSpec((None, s, gd), lambda b_, g: (b_, 0, off + g)),
                  pl.BlockSpec((None, None, gd, gd), lambda b_, g: (layer, g, 0, 0)),
                  pl.BlockSpec((None, 1, gd), lambda b_, g: (layer, 0, g))],
        out_specs=pl.BlockSpec((None, s, gd), lambda b_, g: (b_, 0, g)),
        out_shape=jax.ShapeDtypeStruct((b, s, groups * gd), BF16),
        compiler_params=_params(2), name="pool",
    )(rest, pool_w, pool_scale.reshape(depth, 1, groups * gd))


def _merge_kernel(att_ref, hy_ref, po_ref, wa_ref, wh_ref, wp_ref, ga_ref, gh_ref, gp_ref, o_ref):
    acc = _sigmoid(ga_ref[...].astype(F32)) * _bdot(att_ref[...], wa_ref[...])
    acc = acc + _sigmoid(gh_ref[...].astype(F32)) * _bdot(hy_ref[...], wh_ref[...])
    acc = acc + _sigmoid(gp_ref[...].astype(F32)) * _bdot(po_ref[...], wp_ref[...])
    o_ref[...] = acc.astype(o_ref.dtype)


def _merge(att, hyo, poo, rest, gate_off, w_att_o, w_hy_o, w_pool_o, layer, *, tm, tn):
    r = att.shape[0]
    d = w_att_o.shape[2]
    goff = gate_off // tn
    nd = d // tn

    def a_spec(width):
        return pl.BlockSpec((tm, width), lambda i, j: (i, 0))

    def w_spec(width):
        return pl.BlockSpec((None, width, tn), lambda i, j: (layer, 0, j))

    def g_spec(branch):
        return pl.BlockSpec((tm, tn), lambda i, j: (i, goff + branch * nd + j))

    return pl.pallas_call(
        _merge_kernel, grid=(r // tm, nd),
        in_specs=[a_spec(att.shape[1]), a_spec(hyo.shape[1]), a_spec(poo.shape[1]),
                  w_spec(att.shape[1]), w_spec(hyo.shape[1]), w_spec(poo.shape[1]),
                  g_spec(0), g_spec(1), g_spec(2)],
        out_specs=pl.BlockSpec((tm, tn), lambda i, j: (i, j)),
        out_shape=jax.ShapeDtypeStruct((r, d), BF16),
        compiler_params=_params(2), name="merge",
    )(att, hyo, poo, w_att_o, w_hy_o, w_pool_o, rest, rest, rest)


def _gelu_tanh(x):
    return 0.5 * x * (1.0 + jnp.tanh(math.sqrt(2.0 / math.pi) * (x + 0.044715 * (x * x * x))))


def _ffn_up_kernel(a_ref, wg_ref, wv_ref, cw_ref, o_ref, *, seg):
    a = a_ref[...]
    gate = _dwconv3(_bdot(a, wg_ref[...]), cw_ref[...], seg)
    o_ref[...] = (_gelu_tanh(gate) * _bdot(a, wv_ref[...])).astype(o_ref.dtype)


def _ffn_up(h, w_up, conv_w, layer, seg, *, tn):
    g, s, k = h.shape
    ff = w_up.shape[2] // 2
    nt = ff // tn
    return pl.pallas_call(
        functools.partial(_ffn_up_kernel, seg=seg), grid=(g, nt),
        in_specs=[pl.BlockSpec((None, s, k), lambda a_, j: (a_, 0, 0)),
                  pl.BlockSpec((None, k, tn), lambda a_, j: (layer, 0, j)),
                  pl.BlockSpec((None, k, tn), lambda a_, j: (layer, 0, nt + j)),
                  pl.BlockSpec((None, 3, tn), lambda a_, j: (layer, 0, j))],
        out_specs=pl.BlockSpec((None, s, tn), lambda a_, j: (a_, 0, j)),
        out_shape=jax.ShapeDtypeStruct((g, s, ff), BF16),
        compiler_params=_params(2), name="ffn_up",
    )(h, w_up, w_up, conv_w)


def _rope_tables(seq):
    rows = seq // GRID_W
    row = jnp.repeat(jnp.arange(rows, dtype=F32), GRID_W)
    col = jnp.tile(jnp.arange(GRID_W, dtype=F32), rows)
    n_freq = ATT_HEAD_DIM // 4
    inv = ROPE_BASE ** (-jnp.arange(n_freq, dtype=F32) / n_freq)
    ang = jnp.stack([row[:, None] * inv, col[:, None] * inv], axis=1)
    cos, sin = jnp.cos(ang), jnp.sin(ang)
    zero = jnp.zeros_like(sin)
    co = jnp.stack([cos, cos], axis=2).reshape(seq, ATT_HEAD_DIM)
    sa = jnp.stack([-sin, zero], axis=2).reshape(seq, ATT_HEAD_DIM)
    sb = jnp.stack([zero, sin], axis=2).reshape(seq, ATT_HEAD_DIM)
    rep = LANE // ATT_HEAD_DIM
    return tuple(jnp.tile(t, (1, rep)) for t in (co, sa, sb))


def _dft_matrices(s, hb):
    n_fft = 2 * s
    lo = min(s, 64)
    hi = s // lo
    sp = jnp.arange(s, dtype=jnp.int32)[None, :]
    f_lo = jnp.arange(lo, dtype=jnp.int32)[:, None]
    f_hi = jnp.arange(hi, dtype=jnp.int32)[:, None] * lo
    ang_lo = (2.0 * math.pi / n_fft) * ((f_lo * sp) % n_fft).astype(F32)
    ang_hi = (2.0 * math.pi / n_fft) * ((f_hi * sp) % n_fft).astype(F32)
    c1, s1 = jnp.cos(ang_lo)[None], jnp.sin(ang_lo)[None]
    c2, s2 = jnp.cos(ang_hi)[:, None], jnp.sin(ang_hi)[:, None]
    cmat = (c1 * c2 - s1 * s2).reshape(s, s)
    smat = (s1 * c2 + c1 * s2).reshape(s, s)
    nyq = jnp.where(jnp.arange(s) % 2 == 0, 1.0, -1.0).astype(F32)
    smat = smat.at[0].set(nyq)
    fmat = jnp.concatenate([cmat.reshape(s // hb, hb, s), smat.reshape(s // hb, hb, s)], axis=1)
    fmat = fmat.reshape(n_fft, s).astype(BF16)
    return fmat, fmat.T


def _hy_tables(s, width):
    t = jnp.linspace(0.0, 1.0, s, dtype=F32)[:, None]
    w_ang = 2.0 * math.pi * jnp.arange(s, dtype=F32)[:, None] / s
    f = jnp.linspace(1e-4, HY_BANDS - 1, HY_BANDS, dtype=F32)[None, :]
    z = jnp.concatenate([t, jnp.cos(f * w_ang), -jnp.sin(f * w_ang)], axis=-1)
    z = jnp.pad(z, ((0, 0), (0, LANE - HY_EMB_DIM)))
    deltas = jnp.abs(jnp.linspace(HY_MIN_DECAY, HY_MAX_DECAY, width, dtype=F32))
    return z, jnp.exp(-t * deltas[None, :])


def _hyena_branch(conv, layer, p, tabs):
    width = p["w_hy_o"].shape[1]
    z, dec, fmat, fmat_t, tmf, tmi = tabs
    e, o = _hy_filters(z, dec, p["hy_ffn_w1"], p["hy_ffn_b1"], p["hy_ffn_w2"], p["hy_ffn_b2"],
                       p["hy_ffn_w3"], p["hy_freq"], layer, width)
    kf = _hy_spectrum(fmat, e, o, tmf=tmf)
    depth = p["hy_bias"].shape[0]
    bias = p["hy_bias"].reshape(depth * HY_ORDER, 1, width)
    yf = _hy_fwd(fmat, conv, 0, kf, 0, tmf=tmf)
    zsig = _hy_inv(fmat_t, yf, conv, 0, conv, 1, bias, layer * HY_ORDER, F32, tmi=tmi)
    yf = _hy_fwd(fmat, zsig, 0, kf, 1, tmf=tmf)
    return _hy_inv(fmat_t, yf, zsig, 0, conv, 2, bias, layer * HY_ORDER + 1, BF16, tmi=tmi)


def kernel(x, c, ctx, c_ctx, w_ada, b_ada, norm_g, w_in, diff_lam, attn_subln_g, hy_short_w,
           hy_ffn_w1, hy_ffn_b1, hy_ffn_w2, hy_ffn_b2, hy_ffn_w3, hy_freq, hy_bias, pool_w, pool_scale,
           w_att_o, w_hy_o, w_pool_o, w_out, w_up, ff_conv_w, w_down):
    p = dict(hy_short_w=hy_short_w, hy_ffn_w1=hy_ffn_w1, hy_ffn_b1=hy_ffn_b1, hy_ffn_w2=hy_ffn_w2,
             hy_ffn_b2=hy_ffn_b2, hy_ffn_w3=hy_ffn_w3, hy_freq=hy_freq, hy_bias=hy_bias, w_hy_o=w_hy_o)
    b, seq, d = x.shape
    lc = ctx.shape[1]
    depth = w_ada.shape[0]
    att_w = w_att_o.shape[1]
    hy_w = w_hy_o.shape[1]
    pool_width = w_pool_o.shape[1]
    qkv_w = 3 * att_w
    rest_w = w_in.shape[2] - qkv_w
    gate_off = 3 * hy_w + pool_width

    cond = jnp.concatenate([c, c_ctx[None], jnp.zeros((8 - b - 1, d), F32)], axis=0)
    mod = _ada(cond, w_ada, b_ada)

    rope = _rope_tables(seq)
    tabs_x = _hy_tables(seq, hy_w) + _dft_matrices(seq, 512) + (1024, 1024)
    tabs_c = _hy_tables(lc, hy_w) + _dft_matrices(lc, lc) + (2 * lc, lc)

    ctx = ctx.reshape(1, b * lc, d)

    def mods(layer):
        m = mod[layer].reshape(8, 6, d)
        return [m[:b, k][:, None, :] for k in range(6)], [m[b:b + 1, k][:, None, :] for k in range(6)]

    m_x, m_c = mods(0)
    hx = _resnorm(x, g_norm=norm_g[0, 0], shift=m_x[0], scale=m_x[1])
    hc = _resnorm(ctx, g_norm=norm_g[0, 0], shift=m_c[0], scale=m_c[1])

    for l in range(depth):
        last = l == depth - 1
        lam_init = 0.8 - 0.6 * math.exp(-0.3 * l)
        m_x, m_c = mods(l)
        streams = [("x", x, hx, m_x, seq, b), ("c", ctx, hc, m_c, lc, b)]
        qkv_c = _mm(hc, w_in, l, 0, qkv_w, BF16, tm=b * lc, tn=512, name="mm_qkv").reshape(b, lc, qkv_w)
        qkv_x = _mm(hx, w_in, l, 0, qkv_w, BF16, tm=seq, tn=512, rope=rope, rope_cols=2 * att_w,
                    name="mm_qkv_rope")
        new = {}
        for tag, res, h, m, s_len, nb in streams:
            is_x = tag == "x"
            if last and not is_x:
                continue
            g_rows = res.shape[0] * res.shape[1]
            rest = _mm(h, w_in, l, qkv_w, gate_off, F32, tm=h.shape[1], tn=512, conv=(hy_short_w, s_len),
                       name="mm_rest")
            gates = _mm(h, w_in, l, qkv_w + gate_off, rest_w - gate_off, BF16, tm=h.shape[1], tn=512,
                        name="mm_gates")
            rest_seq = rest.reshape(nb, s_len, gate_off)
            if is_x:
                att = _attention(qkv_x, qkv_c, diff_lam, attn_subln_g, l, lam_init, tq=1024, n_sub=4)
            else:
                att = _attention(qkv_c, None, diff_lam, attn_subln_g, l, lam_init, tq=s_len, n_sub=2)
            hyo = _hyena_branch(rest_seq, l, p, tabs_x if is_x else tabs_c)
            poo = _pool(rest_seq, 3 * hy_w, pool_w, pool_scale, l)
            merged = _merge(att.reshape(g_rows, att_w), hyo.reshape(g_rows, hy_w),
                            poo.reshape(g_rows, pool_width), gates.reshape(g_rows, rest_w - gate_off), 0,
                            w_att_o, w_hy_o, w_pool_o, l, tm=min(g_rows, 2048), tn=256)
            merged = merged.reshape(h.shape)
            res, h2 = _mm_tail(merged, w_out, l, res, norm_g[l, 1], m[2], norm_g[l, 2], m[3], m[4],
                               tm=512, tn=512, name="mm_out_tail")
            act = _ffn_up(h2, w_up, ff_conv_w, l, s_len, tn=256)
            if last:
                res = _mm_tail(act, w_down, l, res, norm_g[l, 3], m[5], tm=512, tn=256, name="mm_down_tail")
                h_next = None
            else:
                m_nx, m_nc = mods(l + 1)
                m_n = m_nx if is_x else m_nc
                res, h_next = _mm_tail(act, w_down, l, res, norm_g[l, 3], m[5], norm_g[l + 1, 0], m_n[0], m_n[1],
                                       tm=512, tn=256, name="mm_down_tail")
            new[tag] = (res, h_next)
        x, hx = new["x"]
        if not last:
            ctx, hc = new["c"]
    return x
```

```python
import functools
import math

import jax
import jax.numpy as jnp
from jax import lax
from jax.experimental import pallas as pl
from jax.experimental.pallas import tpu as pltpu

F32 = jnp.float32
BF16 = jnp.bfloat16

GRID_W = 64
N_BRANCH = 3
ATT_HEAD_DIM = 64
ATT_V_DIM = 2 * ATT_HEAD_DIM
ROPE_BASE = 10000.0
HY_ORDER = 2
HY_BANDS = 16
HY_EMB_DIM = 1 + 2 * HY_BANDS
HY_MIN_DECAY = math.log(1e-2) / 1.5
HY_MAX_DECAY = math.log(1e-2) / 0.3
POOL_WINDOWS = (2, 4, 8, 16)
EPS = 1e-6

V7X_VMEM_BYTES = 64 * 1024 * 1024
VMEM_LIMIT = V7X_VMEM_BYTES - 8 * 1024 * 1024
LANE = 128


def _params(n_axes):
    return pltpu.CompilerParams(dimension_semantics=("arbitrary",) * n_axes,
                                vmem_limit_bytes=VMEM_LIMIT)


def _bdot(a, b):
    return jnp.dot(a.astype(BF16), b.astype(BF16), preferred_element_type=F32)


def _sigmoid(x):
    return 1.0 / (1.0 + jnp.exp(-x))


def _rms(x, g):
    return x * lax.rsqrt(jnp.mean(x * x, axis=-1, keepdims=True) + EPS) * g


def _ada_kernel(c_ref, w_ref, b_ref, o_ref):
    c = c_ref[...]
    o_ref[...] = _bdot(c * _sigmoid(c), w_ref[...]) + b_ref[...]


def _ada(cond, w_ada, b_ada):
    depth, d, n = w_ada.shape
    rows = cond.shape[0]
    tn = 1024
    return pl.pallas_call(
        _ada_kernel,
        grid=(depth, n // tn),
        in_specs=[pl.BlockSpec((rows, d), lambda l, j: (0, 0)),
                  pl.BlockSpec((None, d, tn), lambda l, j: (l, 0, j)),
                  pl.BlockSpec((None, 1, tn), lambda l, j: (l, 0, j))],
        out_specs=pl.BlockSpec((None, rows, tn), lambda l, j: (l, 0, j)),
        out_shape=jax.ShapeDtypeStruct((depth, rows, n), F32),
        compiler_params=_params(2),
        name="ada",
    )(cond, w_ada, b_ada.reshape(depth, 1, n))


def _resnorm_kernel(*refs, has_res, has_norm):
    it = iter(refs)
    x_ref = next(it)
    if has_res:
        y_ref, gres_ref, gate_ref = next(it), next(it), next(it)
    if has_norm:
        gnorm_ref, shift_ref, scale_ref = next(it), next(it), next(it)
    x = x_ref[...]
    if has_res:
        x = x + gate_ref[...] * _rms(y_ref[...], gres_ref[...])
        xo_ref = next(it)
        xo_ref[...] = x
    if has_norm:
        h_ref = next(it)
        h_ref[...] = (_rms(x, gnorm_ref[...]) * (1.0 + scale_ref[...]) + shift_ref[...]).astype(BF16)


def _resnorm(x, y=None, g_res=None, gate=None, g_norm=None, shift=None, scale=None):
    g, s, d = x.shape
    ts = 256
    has_res, has_norm = y is not None, g_norm is not None
    row = pl.BlockSpec((None, ts, d), lambda a, i: (a, i, 0))
    vec = pl.BlockSpec((1, d), lambda a, i: (0, 0))
    mod = pl.BlockSpec((None, 1, d), lambda a, i: (a, 0, 0))
    args, specs, outs, out_specs = [x], [row], [], []
    if has_res:
        args += [y, g_res.reshape(1, d), gate]
        specs += [row, vec, mod]
        outs.append(jax.ShapeDtypeStruct((g, s, d), F32))
        out_specs.append(row)
    if has_norm:
        args += [g_norm.reshape(1, d), shift, scale]
        specs += [vec, mod, mod]
        outs.append(jax.ShapeDtypeStruct((g, s, d), BF16))
        out_specs.append(row)
    res = pl.pallas_call(
        functools.partial(_resnorm_kernel, has_res=has_res, has_norm=has_norm),
        grid=(g, s // ts), in_specs=specs, out_specs=out_specs, out_shape=outs,
        compiler_params=_params(2), name="resnorm",
    )(*args)
    return res if len(res) > 1 else res[0]


def _mm_kernel(a_ref, w_ref, o_ref):
    o_ref[...] = _bdot(a_ref[...], w_ref[...]).astype(o_ref.dtype)


def _mm_rope_kernel(a_ref, w_ref, co_ref, sa_ref, sb_ref, o_ref, *, rope_tiles):
    acc = _bdot(a_ref[...], w_ref[...])
    j = pl.program_id(2)

    @pl.when(j < rope_tiles)
    def _():
        co, sa, sb = co_ref[...], sa_ref[...], sb_ref[...]
        quarter = ATT_HEAD_DIM // 4
        for c in range(acc.shape[1] // LANE):
            t = acc[:, c * LANE:(c + 1) * LANE]
            r = t * co + pltpu.roll(t, LANE - quarter, 1) * sa + pltpu.roll(t, quarter, 1) * sb
            o_ref[:, c * LANE:(c + 1) * LANE] = r.astype(o_ref.dtype)

    @pl.when(j >= rope_tiles)
    def _():
        o_ref[...] = acc.astype(o_ref.dtype)


def _mm_conv_kernel(a_ref, w_ref, cw_ref, o_ref, *, conv_tiles, seg):
    acc = _bdot(a_ref[...], w_ref[...])
    j = pl.program_id(2)

    @pl.when(j < conv_tiles)
    def _():
        o_ref[...] = _dwconv3(acc, cw_ref[...], seg).astype(o_ref.dtype)

    @pl.when(j >= conv_tiles)
    def _():
        o_ref[...] = acc.astype(o_ref.dtype)


def _mm(a, w, layer, col_off, n, out_dtype, *, tm, tn, rope=None, rope_cols=0, conv=None, name="mm"):
    g, s, k = a.shape
    assert col_off % tn == 0 and n % tn == 0 and s % tm == 0
    off = col_off // tn
    in_specs = [pl.BlockSpec((None, tm, k), lambda a_, i, j: (a_, i, 0)),
                pl.BlockSpec((None, k, tn), lambda a_, i, j: (layer, 0, off + j))]
    args = [a, w]
    if conv is not None:
        taps, seg = conv
        conv_tiles = taps.shape[2] // tn
        assert tm % seg == 0 and taps.shape[2] % tn == 0
        kern = functools.partial(_mm_conv_kernel, conv_tiles=conv_tiles, seg=seg)
        in_specs.append(pl.BlockSpec((None, 3, tn), lambda a_, i, j: (layer, 0, jnp.minimum(j, conv_tiles - 1))))
        args.append(taps)
    elif rope is None:
        kern = _mm_kernel
    else:
        kern = functools.partial(_mm_rope_kernel, rope_tiles=rope_cols // tn)
        in_specs += [pl.BlockSpec((tm, LANE), lambda a_, i, j: (i, 0))] * 3
        args += list(rope)
    return pl.pallas_call(
        kern, grid=(g, s // tm, n // tn), in_specs=in_specs,
        out_specs=pl.BlockSpec((None, tm, tn), lambda a_, i, j: (a_, i, j)),
        out_shape=jax.ShapeDtypeStruct((g, s, n), out_dtype),
        compiler_params=_params(3), name=name,
    )(*args)


def _attn_kernel(*refs, has_ctx, lam_init, s_own, s_ctx, n_sub):
    if has_ctx:
        q_ref, k_ref, v_ref, kc_ref, vc_ref, lam_ref, g_ref, o_ref, kk, vv = refs
    else:
        q_ref, k_ref, v_ref, lam_ref, g_ref, o_ref, kk, vv = refs

    @pl.when(pl.program_id(2) == 0)
    def _():
        kk[0:s_own, :] = k_ref[...]
        vv[0:s_own, :] = v_ref[...]
        if has_ctx:
            kk[s_own:s_own + s_ctx, :] = kc_ref[...]
            vv[s_own:s_own + s_ctx, :] = vc_ref[...]

    lp = lam_ref[...]
    lam = (jnp.exp(jnp.sum(lp[0:1] * lp[1:2], axis=-1, keepdims=True))
           - jnp.exp(jnp.sum(lp[2:3] * lp[3:4], axis=-1, keepdims=True)) + lam_init)

    ts = q_ref.shape[0] // n_sub
    for sb in range(n_sub):
        q = q_ref[sb * ts:(sb + 1) * ts, :] * (ATT_HEAD_DIM ** -0.5)
        lane = lax.broadcasted_iota(jnp.int32, q.shape, 1)
        zero = jnp.zeros_like(q)
        q2 = jnp.concatenate([jnp.where(lane < ATT_HEAD_DIM, q, zero),
                              jnp.where(lane >= ATT_HEAD_DIM, q, zero)], axis=0)
        s = lax.dot_general(q2, kk[...], (((1,), (1,)), ((), ())), preferred_element_type=F32)
        m = jnp.max(s, axis=-1, keepdims=True)
        e = jnp.exp(s - m)
        den = jnp.sum(e, axis=-1, keepdims=True)
        o = jnp.dot(e.astype(BF16), vv[...], preferred_element_type=F32) * (1.0 / den)
        att = o[:ts] - lam * o[ts:]
        o_ref[sb * ts:(sb + 1) * ts, :] = (_rms(att, g_ref[...]) * (1.0 - lam_init)).astype(o_ref.dtype)


def _attention(qkv, qkv_ctx, diff_lam, subln_g, layer, lam_init, *, tq, n_sub):
    b, s, w3 = qkv.shape
    hd = ATT_V_DIM
    heads = w3 // (3 * hd)
    has_ctx = qkv_ctx is not None
    s_ctx = qkv_ctx.shape[1] if has_ctx else 0
    in_specs = [pl.BlockSpec((None, tq, hd), lambda b_, h, i: (b_, i, h)),
                pl.BlockSpec((None, s, hd), lambda b_, h, i: (b_, 0, heads + h)),
                pl.BlockSpec((None, s, hd), lambda b_, h, i: (b_, 0, 2 * heads + h))]
    args = [qkv, qkv, qkv]
    if has_ctx:
        in_specs += [pl.BlockSpec((None, s_ctx, hd), lambda b_, h, i: (b_, 0, heads + h)),
                     pl.BlockSpec((None, s_ctx, hd), lambda b_, h, i: (b_, 0, 2 * heads + h))]
        args += [qkv_ctx, qkv_ctx]
    in_specs += [pl.BlockSpec((None, 4, ATT_HEAD_DIM), lambda b_, h, i: (layer, 0, 0)),
                 pl.BlockSpec((None, 1, hd), lambda b_, h, i: (layer, 0, 0))]
    args += [diff_lam, subln_g.reshape(subln_g.shape[0], 1, hd)]
    return pl.pallas_call(
        functools.partial(_attn_kernel, has_ctx=has_ctx, lam_init=lam_init, s_own=s, s_ctx=s_ctx,
                          n_sub=n_sub),
        grid=(b, heads, s // tq), in_specs=in_specs,
        out_specs=pl.BlockSpec((None, tq, hd), lambda b_, h, i: (b_, i, h)),
        out_shape=jax.ShapeDtypeStruct((b, s, heads * hd), BF16),
        scratch_shapes=[pltpu.VMEM((s + s_ctx, hd), BF16), pltpu.VMEM((s + s_ctx, hd), BF16)],
        compiler_params=_params(3), name="attn",
    )(*args)


def _shift_rows(x, d, seg):
    n = x.shape[0]
    if d == 0:
        return x
    rolled = pltpu.roll(x, (-d) % n, 0)
    pos = lax.broadcasted_iota(jnp.int32, x.shape, 0) % seg
    ok = (pos + d >= 0) & (pos + d < seg)
    return jnp.where(ok, rolled, jnp.zeros_like(x))


def _dwconv3(x, w, seg):
    return _shift_rows(x, -1, seg) * w[0:1] + x * w[1:2] + _shift_rows(x, 1, seg) * w[2:3]


def _hdot(a, b):
    return jnp.dot(a, b, preferred_element_type=F32, precision=lax.Precision.HIGHEST)


def _filter_kernel(z_ref, w1_ref, b1_ref, w2_ref, b2_ref, w3f_ref, w3b_ref, fr_ref, dec_ref, e_ref, o_ref):
    fr = fr_ref[...]
    h = jnp.sin(fr[0:1] * (_hdot(z_ref[...], w1_ref[...]) + b1_ref[...]))
    h = jnp.sin(fr[1:2] * (_hdot(h, w2_ref[...]) + b2_ref[...]))
    dec = dec_ref[...]
    fwd = _hdot(h, w3f_ref[...]) * dec
    bwd = _hdot(h, w3b_ref[...]) * dec
    row = lax.broadcasted_iota(jnp.int32, bwd.shape, 0)
    bwd = jnp.where(row == 0, jnp.zeros_like(bwd), bwd)
    norm = jnp.sum(jnp.abs(fwd), axis=0, keepdims=True) + jnp.sum(jnp.abs(bwd), axis=0, keepdims=True) + EPS
    e_ref[...] = (fwd + bwd) / norm
    o_ref[...] = (fwd - bwd) / norm


def _hy_filters(z, dec, w1, b1, w2, b2, w3, freq, n_layers, width):
    s = z.shape[0]
    depth, emb, fd = w1.shape
    w1p = jnp.pad(w1, ((0, 0), (0, z.shape[1] - emb), (0, 0)))
    tc = 256
    nct = width // tc
    n_oc = HY_ORDER * nct
    return pl.pallas_call(
        _filter_kernel, grid=(n_layers, n_oc),
        in_specs=[pl.BlockSpec((s, z.shape[1]), lambda l, j: (0, 0)),
                  pl.BlockSpec((None, z.shape[1], fd), lambda l, j: (l, 0, 0)),
                  pl.BlockSpec((None, 1, fd), lambda l, j: (l, 0, 0)),
                  pl.BlockSpec((None, fd, fd), lambda l, j: (l, 0, 0)),
                  pl.BlockSpec((None, 1, fd), lambda l, j: (l, 0, 0)),
                  pl.BlockSpec((None, fd, tc), lambda l, j: (l, 0, j)),
                  pl.BlockSpec((None, fd, tc), lambda l, j: (l, 0, n_oc + j)),
                  pl.BlockSpec((None, 2, fd), lambda l, j: (l, 0, 0)),
                  pl.BlockSpec((s, tc), lambda l, j: (0, j % nct))],
        out_specs=[pl.BlockSpec((None, s, tc), lambda l, j: (l, 0, j))] * 2,
        out_shape=[jax.ShapeDtypeStruct((n_layers, s, HY_ORDER * width), F32)] * 2,
        compiler_params=_params(2), name="hy_filter",
    )(z, w1p, b1.reshape(depth, 1, fd), w2, b2.reshape(depth, 1, fd), w3, w3, freq, dec)


def _spectrum_kernel(f_ref, e_ref, o_ref, k_ref, *, n_fft):
    hb = f_ref.shape[0] // 2
    e = e_ref[...].astype(BF16)
    acc_e = _bdot(f_ref[0:hb, :], e)
    acc_o = _bdot(f_ref[hb:, :], o_ref[...])
    nyq = _bdot(f_ref[hb:hb + 16, :], e)[0:1]
    first = lax.broadcasted_iota(jnp.int32, acc_e.shape, 0) + pl.program_id(0) * hb == 0
    scale = jnp.where(first, 1.0 / n_fft, 2.0 / n_fft)
    k_ref[0:hb, :] = acc_e * scale
    k_ref[hb:, :] = jnp.where(first, nyq, acc_o) * scale


def _hy_spectrum(fmat, e, o, *, tmf):
    n_fft, s = fmat.shape
    n_layers, _, n = e.shape
    tn = 512
    return pl.pallas_call(
        functools.partial(_spectrum_kernel, n_fft=n_fft), grid=(n_fft // tmf, n_layers, n // tn),
        in_specs=[pl.BlockSpec((tmf, s), lambda i, l, j: (i, 0)),
                  pl.BlockSpec((None, s, tn), lambda i, l, j: (l, 0, j)),
                  pl.BlockSpec((None, s, tn), lambda i, l, j: (l, 0, j))],
        out_specs=pl.BlockSpec((None, tmf, tn), lambda i, l, j: (l, i, j)),
        out_shape=jax.ShapeDtypeStruct((n_layers, n_fft, n), F32),
        compiler_params=_params(3), name="hy_spectrum",
    )(fmat, e, o)


def _fwd_dft_kernel(f_ref, u_ref, k_ref, y_ref):
    acc = _bdot(f_ref[...], u_ref[...])
    hb = acc.shape[0] // 2
    ua, ub = acc[:hb], acc[hb:]
    ka, kb = k_ref[0:hb, :], k_ref[hb:, :]
    first = lax.broadcasted_iota(jnp.int32, ua.shape, 0) + pl.program_id(0) * hb == 0
    ya = ua * ka - jnp.where(first, jnp.zeros_like(ub), ub * kb)
    yb = jnp.where(first, ub * kb, ua * kb + ub * ka)
    y_ref[0:hb, :] = ya.astype(y_ref.dtype)
    y_ref[hb:, :] = yb.astype(y_ref.dtype)


def _hy_fwd(fmat, u, ucol, kf, layer, order, *, tmf):
    n_fft, s = fmat.shape
    b = u.shape[0]
    c = kf.shape[2] // HY_ORDER
    return pl.pallas_call(
        _fwd_dft_kernel, grid=(n_fft // tmf, b),
        in_specs=[pl.BlockSpec((tmf, s), lambda i, b_: (i, 0)),
                  pl.BlockSpec((None, s, c), lambda i, b_: (b_, 0, ucol)),
                  pl.BlockSpec((None, tmf, c), lambda i, b_: (layer, i, order))],
        out_specs=pl.BlockSpec((None, tmf, c), lambda i, b_: (b_, i, 0)),
        out_shape=jax.ShapeDtypeStruct((b, n_fft, c), BF16),
        compiler_params=_params(2), name="hy_fwd",
    )(fmat, u, kf)


def _inv_dft_kernel(ft_ref, y_ref, u_ref, g_ref, bias_ref, o_ref):
    conv = jnp.dot(ft_ref[...], y_ref[...], preferred_element_type=F32)
    o_ref[...] = (g_ref[...] * (conv + u_ref[...] * bias_ref[...])).astype(o_ref.dtype)


def _hy_inv(fmat_t, yf, u, ucol, gate, gcol, hy_bias, bias_row, out_dtype, *, tmi):
    s, n_fft = fmat_t.shape
    b, _, c = yf.shape
    return pl.pallas_call(
        _inv_dft_kernel, grid=(s // tmi, b),
        in_specs=[pl.BlockSpec((tmi, n_fft), lambda i, b_: (i, 0)),
                  pl.BlockSpec((None, n_fft, c), lambda i, b_: (b_, 0, 0)),
                  pl.BlockSpec((None, tmi, c), lambda i, b_: (b_, i, ucol)),
                  pl.BlockSpec((None, tmi, c), lambda i, b_: (b_, i, gcol)),
                  pl.BlockSpec((None, 1, c), lambda i, b_: (bias_row, 0, 0))],
        out_specs=pl.BlockSpec((None, tmi, c), lambda i, b_: (b_, i, 0)),
        out_shape=jax.ShapeDtypeStruct((b, s, c), out_dtype),
        compiler_params=_params(2), name="hy_inv",
    )(fmat_t, yf, u, gate, hy_bias)


def _pool_kernel(x_ref, w_ref, sc_ref, o_ref):
    x = x_ref[...]
    s = x.shape[0]
    pos = lax.broadcasted_iota(jnp.int32, x.shape, 0)
    grp = pl.program_id(1)
    for gi, win in enumerate(POOL_WINDOWS):
        @pl.when(grp == gi)
        def _(win=win):
            half = win // 2
            tot = x
            for d in range(-half, win - half):
                if d != 0:
                    tot = tot + _shift_rows(x, d, s)
            cnt = jnp.minimum(pos + (win - half), s) - jnp.maximum(pos - half, 0)
            mean = tot / cnt.astype(F32)
            o_ref[...] = (_bdot(mean - x, w_ref[...]) * sc_ref[...]).astype(o_ref.dtype)


def _pool(rest, col_off, pool_w, pool_scale, layer):
    b, s, _ = rest.shape
    depth, groups, gd, _ = pool_w.shape
    off = col_off // gd
    return pl.pallas_call(
        _pool_kernel, grid=(b, groups),
        in_specs=[pl.BlockSpec((None, s, gd), lambda b_, g: (b_, 0, off + g)),
                  pl.BlockSpec((None, None, gd, gd), lambda b_, g: (layer, g, 0, 0)),
                  pl.BlockSpec((None, 1, gd), lambda b_, g: (layer, 0, g))],
        out_specs=pl.BlockSpec((None, s, gd), lambda b_, g: (b_, 0, g)),
        out_shape=jax.ShapeDtypeStruct((b, s, groups * gd), BF16),
        compiler_params=_params(2), name="pool",
    )(rest, pool_w, pool_scale.reshape(depth, 1, groups * gd))


def _merge_kernel(att_ref, hy_ref, po_ref, wa_ref, wh_ref, wp_ref, ga_ref, gh_ref, gp_ref, o_ref):
    acc = _sigmoid(ga_ref[...]) * _bdot(att_ref[...], wa_ref[...])
    acc = acc + _sigmoid(gh_ref[...]) * _bdot(hy_ref[...], wh_ref[...])
    acc = acc + _sigmoid(gp_ref[...]) * _bdot(po_ref[...], wp_ref[...])
    o_ref[...] = acc.astype(o_ref.dtype)


def _merge(att, hyo, poo, rest, gate_off, w_att_o, w_hy_o, w_pool_o, layer, *, tm, tn):
    r = att.shape[0]
    d = w_att_o.shape[2]
    goff = gate_off // tn
    nd = d // tn

    def a_spec(width):
        return pl.BlockSpec((tm, width), lambda i, j: (i, 0))

    def w_spec(width):
        return pl.BlockSpec((None, width, tn), lambda i, j: (layer, 0, j))

    def g_spec(branch):
        return pl.BlockSpec((tm, tn), lambda i, j: (i, goff + branch * nd + j))

    return pl.pallas_call(
        _merge_kernel, grid=(r // tm, nd),
        in_specs=[a_spec(att.shape[1]), a_spec(hyo.shape[1]), a_spec(poo.shape[1]),
                  w_spec(att.shape[1]), w_spec(hyo.shape[1]), w_spec(poo.shape[1]),
                  g_spec(0), g_spec(1), g_spec(2)],
        out_specs=pl.BlockSpec((tm, tn), lambda i, j: (i, j)),
        out_shape=jax.ShapeDtypeStruct((r, d), BF16),
        compiler_params=_params(2), name="merge",
    )(att, hyo, poo, w_att_o, w_hy_o, w_pool_o, rest, rest, rest)


def _gelu_tanh(x):
    return 0.5 * x * (1.0 + jnp.tanh(math.sqrt(2.0 / math.pi) * (x + 0.044715 * (x * x * x))))


def _ffn_up_kernel(a_ref, wg_ref, wv_ref, cw_ref, o_ref, *, seg):
    a = a_ref[...]
    gate = _dwconv3(_bdot(a, wg_ref[...]), cw_ref[...], seg)
    o_ref[...] = (_gelu_tanh(gate) * _bdot(a, wv_ref[...])).astype(o_ref.dtype)


def _ffn_up(h, w_up, conv_w, layer, seg, *, tn):
    g, s, k = h.shape
    ff = w_up.shape[2] // 2
    nt = ff // tn
    return pl.pallas_call(
        functools.partial(_ffn_up_kernel, seg=seg), grid=(g, nt),
        in_specs=[pl.BlockSpec((None, s, k), lambda a_, j: (a_, 0, 0)),
                  pl.BlockSpec((None, k, tn), lambda a_, j: (layer, 0, j)),
                  pl.BlockSpec((None, k, tn), lambda a_, j: (layer, 0, nt + j)),
                  pl.BlockSpec((None, 3, tn), lambda a_, j: (layer, 0, j))],
        out_specs=pl.BlockSpec((None, s, tn), lambda a_, j: (a_, 0, j)),
        out_shape=jax.ShapeDtypeStruct((g, s, ff), BF16),
        compiler_params=_params(2), name="ffn_up",
    )(h, w_up, w_up, conv_w)


def _rope_tables(seq):
    rows = seq // GRID_W
    row = jnp.repeat(jnp.arange(rows, dtype=F32), GRID_W)
    col = jnp.tile(jnp.arange(GRID_W, dtype=F32), rows)
    n_freq = ATT_HEAD_DIM // 4
    inv = ROPE_BASE ** (-jnp.arange(n_freq, dtype=F32) / n_freq)
    ang = jnp.stack([row[:, None] * inv, col[:, None] * inv], axis=1)
    cos, sin = jnp.cos(ang), jnp.sin(ang)
    zero = jnp.zeros_like(sin)
    co = jnp.stack([cos, cos], axis=2).reshape(seq, ATT_HEAD_DIM)
    sa = jnp.stack([-sin, zero], axis=2).reshape(seq, ATT_HEAD_DIM)
    sb = jnp.stack([zero, sin], axis=2).reshape(seq, ATT_HEAD_DIM)
    rep = LANE // ATT_HEAD_DIM
    return tuple(jnp.tile(t, (1, rep)) for t in (co, sa, sb))


def _dft_matrices(s, hb):
    n_fft = 2 * s
    lo = min(s, 64)
    hi = s // lo
    sp = jnp.arange(s, dtype=jnp.int32)[None, :]
    f_lo = jnp.arange(lo, dtype=jnp.int32)[:, None]
    f_hi = jnp.arange(hi, dtype=jnp.int32)[:, None] * lo
    ang_lo = (2.0 * math.pi / n_fft) * ((f_lo * sp) % n_fft).astype(F32)
    ang_hi = (2.0 * math.pi / n_fft) * ((f_hi * sp) % n_fft).astype(F32)
    c1, s1 = jnp.cos(ang_lo)[None], jnp.sin(ang_lo)[None]
    c2, s2 = jnp.cos(ang_hi)[:, None], jnp.sin(ang_hi)[:, None]
    cmat = (c1 * c2 - s1 * s2).reshape(s, s)
    smat = (s1 * c2 + c1 * s2).reshape(s, s)
    nyq = jnp.where(jnp.arange(s) % 2 == 0, 1.0, -1.0).astype(F32)
    smat = smat.at[0].set(nyq)
    fmat = jnp.concatenate([cmat.reshape(s // hb, hb, s), smat.reshape(s // hb, hb, s)], axis=1)
    fmat = fmat.reshape(n_fft, s).astype(BF16)
    return fmat, fmat.T


def _hy_tables(s, width):
    t = jnp.linspace(0.0, 1.0, s, dtype=F32)[:, None]
    w_ang = 2.0 * math.pi * jnp.arange(s, dtype=F32)[:, None] / s
    f = jnp.linspace(1e-4, HY_BANDS - 1, HY_BANDS, dtype=F32)[None, :]
    z = jnp.concatenate([t, jnp.cos(f * w_ang), -jnp.sin(f * w_ang)], axis=-1)
    z = jnp.pad(z, ((0, 0), (0, LANE - HY_EMB_DIM)))
    deltas = jnp.abs(jnp.linspace(HY_MIN_DECAY, HY_MAX_DECAY, width, dtype=F32))
    return z, jnp.exp(-t * deltas[None, :])


def _hyena_setup(s, hb, tmf, tmi, n_layers, p):
    width = p["w_hy_o"].shape[1]
    z, dec = _hy_tables(s, width)
    fmat, fmat_t = _dft_matrices(s, hb)
    e, o = _hy_filters(z, dec, p["hy_ffn_w1"], p["hy_ffn_b1"], p["hy_ffn_w2"], p["hy_ffn_b2"],
                       p["hy_ffn_w3"], p["hy_freq"], n_layers, width)
    return fmat, fmat_t, tmf, tmi, _hy_spectrum(fmat, e, o, tmf=tmf)


def _hyena_branch(conv, layer, p, setup):
    width = p["w_hy_o"].shape[1]
    fmat, fmat_t, tmf, tmi, kf = setup
    depth = p["hy_bias"].shape[0]
    bias = p["hy_bias"].reshape(depth * HY_ORDER, 1, width)
    yf = _hy_fwd(fmat, conv, 0, kf, layer, 0, tmf=tmf)
    zsig = _hy_inv(fmat_t, yf, conv, 0, conv, 1, bias, layer * HY_ORDER, F32, tmi=tmi)
    yf = _hy_fwd(fmat, zsig, 0, kf, layer, 1, tmf=tmf)
    return _hy_inv(fmat_t, yf, zsig, 0, conv, 2, bias, layer * HY_ORDER + 1, BF16, tmi=tmi)


def kernel(x, c, ctx, c_ctx, w_ada, b_ada, norm_g, w_in, diff_lam, attn_subln_g, hy_short_w,
           hy_ffn_w1, hy_ffn_b1, hy_ffn_w2, hy_ffn_b2, hy_ffn_w3, hy_freq, hy_bias, pool_w, pool_scale,
           w_att_o, w_hy_o, w_pool_o, w_out, w_up, ff_conv_w, w_down):
    p = dict(hy_short_w=hy_short_w, hy_ffn_w1=hy_ffn_w1, hy_ffn_b1=hy_ffn_b1, hy_ffn_w2=hy_ffn_w2,
             hy_ffn_b2=hy_ffn_b2, hy_ffn_w3=hy_ffn_w3, hy_freq=hy_freq, hy_bias=hy_bias, w_hy_o=w_hy_o)
    b, seq, d = x.shape
    lc = ctx.shape[1]
    depth = w_ada.shape[0]
    att_w = w_att_o.shape[1]
    hy_w = w_hy_o.shape[1]
    pool_width = w_pool_o.shape[1]
    qkv_w = 3 * att_w
    rest_w = w_in.shape[2] - qkv_w
    gate_off = 3 * hy_w + pool_width

    cond = jnp.concatenate([c, c_ctx[None], jnp.zeros((8 - b - 1, d), F32)], axis=0)
    mod = _ada(cond, w_ada, b_ada)

    rope = _rope_tables(seq)
    tabs_x = _hyena_setup(seq, 512, 1024, 1024, depth, p)
    tabs_c = _hyena_setup(lc, lc, 2 * lc, lc, depth - 1, p)

    ctx = ctx.reshape(1, b * lc, d)

    def mods(layer):
        m = mod[layer].reshape(8, 6, d)
        return [m[:b, k][:, None, :] for k in range(6)], [m[b:b + 1, k][:, None, :] for k in range(6)]

    m_x, m_c = mods(0)
    hx = _resnorm(x, g_norm=norm_g[0, 0], shift=m_x[0], scale=m_x[1])
    hc = _resnorm(ctx, g_norm=norm_g[0, 0], shift=m_c[0], scale=m_c[1])

    for l in range(depth):
        last = l == depth - 1
        lam_init = 0.8 - 0.6 * math.exp(-0.3 * l)
        m_x, m_c = mods(l)
        streams = [("x", x, hx, m_x, seq, b), ("c", ctx, hc, m_c, lc, b)]
        qkv_c = _mm(hc, w_in, l, 0, qkv_w, BF16, tm=b * lc, tn=512, name="mm_qkv").reshape(b, lc, qkv_w)
        qkv_x = _mm(hx, w_in, l, 0, qkv_w, BF16, tm=seq, tn=512, rope=rope, rope_cols=2 * att_w,
                    name="mm_qkv_rope")
        new = {}
        for tag, res, h, m, s_len, nb in streams:
            is_x = tag == "x"
            if last and not is_x:
                continue
            g_rows = res.shape[0] * res.shape[1]
            rest = _mm(h, w_in, l, qkv_w, rest_w, F32, tm=h.shape[1], tn=512, conv=(hy_short_w, s_len),
                       name="mm_rest")
            rest_seq = rest.reshape(nb, s_len, rest_w)
            if is_x:
                att = _attention(qkv_x, qkv_c, diff_lam, attn_subln_g, l, lam_init, tq=1024, n_sub=4)
            else:
                att = _attention(qkv_c, None, diff_lam, attn_subln_g, l, lam_init, tq=s_len, n_sub=2)
            hyo = _hyena_branch(rest_seq, l, p, tabs_x if is_x else tabs_c)
            poo = _pool(rest_seq, 3 * hy_w, pool_w, pool_scale, l)
            merged = _merge(att.reshape(g_rows, att_w), hyo.reshape(g_rows, hy_w),
                            poo.reshape(g_rows, pool_width), rest.reshape(g_rows, rest_w), gate_off,
                            w_att_o, w_hy_o, w_pool_o, l, tm=min(g_rows, 2048), tn=256)
            merged = merged.reshape(h.shape)
            mix = _mm(merged, w_out, l, 0, d, F32, tm=h.shape[1], tn=512, name="mm_out")
            res, h2 = _resnorm(res, mix, norm_g[l, 1], m[2], norm_g[l, 2], m[3], m[4])
            act = _ffn_up(h2, w_up, ff_conv_w, l, s_len, tn=256)
            ffn = _mm(act, w_down, l, 0, d, F32, tm=1024, tn=256, name="mm_down")
            if last:
                res = _resnorm(res, ffn, norm_g[l, 3], m[5])
                h_next = None
            else:
                m_nx, m_nc = mods(l + 1)
                m_n = m_nx if is_x else m_nc
                res, h_next = _resnorm(res, ffn, norm_g[l, 3], m[5], norm_g[l + 1, 0], m_n[0], m_n[1])
            new[tag] = (res, h_next)
        x, hx = new["x"]
        if not last:
            ctx, hc = new["c"]
    return x
```

```python
import functools
import math

import jax
import jax.numpy as jnp
from jax import lax
from jax.experimental import pallas as pl
from jax.experimental.pallas import tpu as pltpu

F32 = jnp.float32
BF16 = jnp.bfloat16

GRID_W = 64
N_BRANCH = 3
ATT_HEAD_DIM = 64
ATT_V_DIM = 2 * ATT_HEAD_DIM
ROPE_BASE = 10000.0
HY_ORDER = 2
HY_BANDS = 16
HY_EMB_DIM = 1 + 2 * HY_BANDS
HY_MIN_DECAY = math.log(1e-2) / 1.5
HY_MAX_DECAY = math.log(1e-2) / 0.3
POOL_WINDOWS = (2, 4, 8, 16)
EPS = 1e-6

V7X_VMEM_BYTES = 64 * 1024 * 1024
VMEM_LIMIT = V7X_VMEM_BYTES - 8 * 1024 * 1024
LANE = 128


def _params(n_axes):
    return pltpu.CompilerParams(dimension_semantics=("arbitrary",) * n_axes,
                                vmem_limit_bytes=VMEM_LIMIT)


def _bdot(a, b):
    return jnp.dot(a.astype(BF16), b.astype(BF16), preferred_element_type=F32)


def _sigmoid(x):
    return 1.0 / (1.0 + jnp.exp(-x))


def _rms(x, g):
    return x * lax.rsqrt(jnp.mean(x * x, axis=-1, keepdims=True) + EPS) * g


def _ada_kernel(c_ref, w_ref, b_ref, o_ref):
    c = c_ref[...]
    o_ref[...] = _bdot(c * _sigmoid(c), w_ref[...]) + b_ref[...]


def _ada(cond, w_ada, b_ada):
    depth, d, n = w_ada.shape
    rows = cond.shape[0]
    tn = 1024
    return pl.pallas_call(
        _ada_kernel,
        grid=(depth, n // tn),
        in_specs=[pl.BlockSpec((rows, d), lambda l, j: (0, 0)),
                  pl.BlockSpec((None, d, tn), lambda l, j: (l, 0, j)),
                  pl.BlockSpec((None, 1, tn), lambda l, j: (l, 0, j))],
        out_specs=pl.BlockSpec((None, rows, tn), lambda l, j: (l, 0, j)),
        out_shape=jax.ShapeDtypeStruct((depth, rows, n), F32),
        compiler_params=_params(2),
        name="ada",
    )(cond, w_ada, b_ada.reshape(depth, 1, n))


def _resnorm_kernel(*refs, has_res, has_norm):
    it = iter(refs)
    x_ref = next(it)
    if has_res:
        y_ref, gres_ref, gate_ref = next(it), next(it), next(it)
    if has_norm:
        gnorm_ref, shift_ref, scale_ref = next(it), next(it), next(it)
    x = x_ref[...]
    if has_res:
        x = x + gate_ref[...] * _rms(y_ref[...], gres_ref[...])
        xo_ref = next(it)
        xo_ref[...] = x
    if has_norm:
        h_ref = next(it)
        h_ref[...] = (_rms(x, gnorm_ref[...]) * (1.0 + scale_ref[...]) + shift_ref[...]).astype(BF16)


def _resnorm(x, y=None, g_res=None, gate=None, g_norm=None, shift=None, scale=None):
    g, s, d = x.shape
    ts = 256
    has_res, has_norm = y is not None, g_norm is not None
    row = pl.BlockSpec((None, ts, d), lambda a, i: (a, i, 0))
    vec = pl.BlockSpec((1, d), lambda a, i: (0, 0))
    mod = pl.BlockSpec((None, 1, d), lambda a, i: (a, 0, 0))
    args, specs, outs, out_specs = [x], [row], [], []
    if has_res:
        args += [y, g_res.reshape(1, d), gate]
        specs += [row, vec, mod]
        outs.append(jax.ShapeDtypeStruct((g, s, d), F32))
        out_specs.append(row)
    if has_norm:
        args += [g_norm.reshape(1, d), shift, scale]
        specs += [vec, mod, mod]
        outs.append(jax.ShapeDtypeStruct((g, s, d), BF16))
        out_specs.append(row)
    res = pl.pallas_call(
        functools.partial(_resnorm_kernel, has_res=has_res, has_norm=has_norm),
        grid=(g, s // ts), in_specs=specs, out_specs=out_specs, out_shape=outs,
        compiler_params=_params(2), name="resnorm",
    )(*args)
    return res if len(res) > 1 else res[0]


def _mm_kernel(a_ref, w_ref, o_ref):
    o_ref[...] = _bdot(a_ref[...], w_ref[...]).astype(o_ref.dtype)


def _mm_rope_kernel(a_ref, w_ref, co_ref, sa_ref, sb_ref, o_ref, *, rope_tiles):
    acc = _bdot(a_ref[...], w_ref[...])
    j = pl.program_id(2)

    @pl.when(j < rope_tiles)
    def _():
        co, sa, sb = co_ref[...], sa_ref[...], sb_ref[...]
        quarter = ATT_HEAD_DIM // 4
        for c in range(acc.shape[1] // LANE):
            t = acc[:, c * LANE:(c + 1) * LANE]
            r = t * co + pltpu.roll(t, LANE - quarter, 1) * sa + pltpu.roll(t, quarter, 1) * sb
            o_ref[:, c * LANE:(c + 1) * LANE] = r.astype(o_ref.dtype)

    @pl.when(j >= rope_tiles)
    def _():
        o_ref[...] = acc.astype(o_ref.dtype)


def _mm_conv_kernel(a_ref, w_ref, cw_ref, o_ref, *, conv_tiles, seg):
    acc = _bdot(a_ref[...], w_ref[...])
    j = pl.program_id(2)

    @pl.when(j < conv_tiles)
    def _():
        o_ref[...] = _dwconv3(acc, cw_ref[...], seg).astype(o_ref.dtype)

    @pl.when(j >= conv_tiles)
    def _():
        o_ref[...] = acc.astype(o_ref.dtype)


def _mm(a, w, layer, col_off, n, out_dtype, *, tm, tn, rope=None, rope_cols=0, conv=None, name="mm"):
    g, s, k = a.shape
    assert col_off % tn == 0 and n % tn == 0 and s % tm == 0
    off = col_off // tn
    in_specs = [pl.BlockSpec((None, tm, k), lambda a_, i, j: (a_, i, 0)),
                pl.BlockSpec((None, k, tn), lambda a_, i, j: (layer, 0, off + j))]
    args = [a, w]
    if conv is not None:
        taps, seg = conv
        conv_tiles = taps.shape[2] // tn
        assert tm % seg == 0 and taps.shape[2] % tn == 0
        kern = functools.partial(_mm_conv_kernel, conv_tiles=conv_tiles, seg=seg)
        in_specs.append(pl.BlockSpec((None, 3, tn), lambda a_, i, j: (layer, 0, jnp.minimum(j, conv_tiles - 1))))
        args.append(taps)
    elif rope is None:
        kern = _mm_kernel
    else:
        kern = functools.partial(_mm_rope_kernel, rope_tiles=rope_cols // tn)
        in_specs += [pl.BlockSpec((tm, LANE), lambda a_, i, j: (i, 0))] * 3
        args += list(rope)
    return pl.pallas_call(
        kern, grid=(g, s // tm, n // tn), in_specs=in_specs,
        out_specs=pl.BlockSpec((None, tm, tn), lambda a_, i, j: (a_, i, j)),
        out_shape=jax.ShapeDtypeStruct((g, s, n), out_dtype),
        compiler_params=_params(3), name=name,
    )(*args)


def _attn_kernel(*refs, has_ctx, lam_init, s_own, s_ctx, n_sub):
    if has_ctx:
        q_ref, k_ref, v_ref, kc_ref, vc_ref, lam_ref, g_ref, o_ref, kk, vv = refs
    else:
        q_ref, k_ref, v_ref, lam_ref, g_ref, o_ref, kk, vv = refs

    @pl.when(pl.program_id(2) == 0)
    def _():
        kk[0:s_own, :] = k_ref[...]
        vv[0:s_own, :] = v_ref[...]
        if has_ctx:
            kk[s_own:s_own + s_ctx, :] = kc_ref[...]
            vv[s_own:s_own + s_ctx, :] = vc_ref[...]

    lp = lam_ref[...]
    lam = (jnp.exp(jnp.sum(lp[0:1] * lp[1:2], axis=-1, keepdims=True))
           - jnp.exp(jnp.sum(lp[2:3] * lp[3:4], axis=-1, keepdims=True)) + lam_init)

    ts = q_ref.shape[0] // n_sub
    for sb in range(n_sub):
        q = q_ref[sb * ts:(sb + 1) * ts, :] * (ATT_HEAD_DIM ** -0.5)
        lane = lax.broadcasted_iota(jnp.int32, q.shape, 1)
        zero = jnp.zeros_like(q)
        q2 = jnp.concatenate([jnp.where(lane < ATT_HEAD_DIM, q, zero),
                              jnp.where(lane >= ATT_HEAD_DIM, q, zero)], axis=0)
        s = lax.dot_general(q2, kk[...], (((1,), (1,)), ((), ())), preferred_element_type=F32)
        m = jnp.max(s, axis=-1, keepdims=True)
        e = jnp.exp(s - m)
        den = jnp.sum(e, axis=-1, keepdims=True)
        o = jnp.dot(e.astype(BF16), vv[...], preferred_element_type=F32) * (1.0 / den)
        att = o[:ts] - lam * o[ts:]
        o_ref[sb * ts:(sb + 1) * ts, :] = (_rms(att, g_ref[...]) * (1.0 - lam_init)).astype(o_ref.dtype)


def _attention(qkv, qkv_ctx, diff_lam, subln_g, layer, lam_init, *, tq, n_sub):
    b, s, w3 = qkv.shape
    hd = ATT_V_DIM
    heads = w3 // (3 * hd)
    has_ctx = qkv_ctx is not None
    s_ctx = qkv_ctx.shape[1] if has_ctx else 0
    in_specs = [pl.BlockSpec((None, tq, hd), lambda b_, h, i: (b_, i, h)),
                pl.BlockSpec((None, s, hd), lambda b_, h, i: (b_, 0, heads + h)),
                pl.BlockSpec((None, s, hd), lambda b_, h, i: (b_, 0, 2 * heads + h))]
    args = [qkv, qkv, qkv]
    if has_ctx:
        in_specs += [pl.BlockSpec((None, s_ctx, hd), lambda b_, h, i: (b_, 0, heads + h)),
                     pl.BlockSpec((None, s_ctx, hd), lambda b_, h, i: (b_, 0, 2 * heads + h))]
        args += [qkv_ctx, qkv_ctx]
    in_specs += [pl.BlockSpec((None, 4, ATT_HEAD_DIM), lambda b_, h, i: (layer, 0, 0)),
                 pl.BlockSpec((None, 1, hd), lambda b_, h, i: (layer, 0, 0))]
    args += [diff_lam, subln_g.reshape(subln_g.shape[0], 1, hd)]
    return pl.pallas_call(
        functools.partial(_attn_kernel, has_ctx=has_ctx, lam_init=lam_init, s_own=s, s_ctx=s_ctx,
                          n_sub=n_sub),
        grid=(b, heads, s // tq), in_specs=in_specs,
        out_specs=pl.BlockSpec((None, tq, hd), lambda b_, h, i: (b_, i, h)),
        out_shape=jax.ShapeDtypeStruct((b, s, heads * hd), BF16),
        scratch_shapes=[pltpu.VMEM((s + s_ctx, hd), BF16), pltpu.VMEM((s + s_ctx, hd), BF16)],
        compiler_params=_params(3), name="attn",
    )(*args)


def _shift_rows(x, d, seg):
    n = x.shape[0]
    if d == 0:
        return x
    rolled = pltpu.roll(x, (-d) % n, 0)
    pos = lax.broadcasted_iota(jnp.int32, x.shape, 0) % seg
    ok = (pos + d >= 0) & (pos + d < seg)
    return jnp.where(ok, rolled, jnp.zeros_like(x))


def _dwconv3(x, w, seg):
    return _shift_rows(x, -1, seg) * w[0:1] + x * w[1:2] + _shift_rows(x, 1, seg) * w[2:3]


def _hdot(a, b):
    return jnp.dot(a, b, preferred_element_type=F32, precision=lax.Precision.HIGHEST)


def _filter_kernel(z_ref, w1_ref, b1_ref, w2_ref, b2_ref, w3f_ref, w3b_ref, fr_ref, dec_ref, e_ref, o_ref, h_scr):
    @pl.when(pl.program_id(1) == 0)
    def _():
        fr = fr_ref[...]
        h1 = jnp.sin(fr[0:1] * (_hdot(z_ref[...], w1_ref[...]) + b1_ref[...]))
        h_scr[...] = jnp.sin(fr[1:2] * (_hdot(h1, w2_ref[...]) + b2_ref[...]))

    h = h_scr[...]
    dec = dec_ref[...]
    fwd = _hdot(h, w3f_ref[...]) * dec
    bwd = _hdot(h, w3b_ref[...]) * dec
    row = lax.broadcasted_iota(jnp.int32, bwd.shape, 0)
    bwd = jnp.where(row == 0, jnp.zeros_like(bwd), bwd)
    norm = jnp.sum(jnp.abs(fwd), axis=0, keepdims=True) + jnp.sum(jnp.abs(bwd), axis=0, keepdims=True) + EPS
    e_ref[...] = (fwd + bwd) / norm
    o_ref[...] = (fwd - bwd) / norm


def _hy_filters(z, dec, w1, b1, w2, b2, w3, freq, n_layers, width):
    s = z.shape[0]
    depth, emb, fd = w1.shape
    w1p = jnp.pad(w1, ((0, 0), (0, z.shape[1] - emb), (0, 0)))
    tc = 256
    nct = width // tc
    n_oc = HY_ORDER * nct
    return pl.pallas_call(
        _filter_kernel, grid=(n_layers, n_oc),
        in_specs=[pl.BlockSpec((s, z.shape[1]), lambda l, j: (0, 0)),
                  pl.BlockSpec((None, z.shape[1], fd), lambda l, j: (l, 0, 0)),
                  pl.BlockSpec((None, 1, fd), lambda l, j: (l, 0, 0)),
                  pl.BlockSpec((None, fd, fd), lambda l, j: (l, 0, 0)),
                  pl.BlockSpec((None, 1, fd), lambda l, j: (l, 0, 0)),
                  pl.BlockSpec((None, fd, tc), lambda l, j: (l, 0, j)),
                  pl.BlockSpec((None, fd, tc), lambda l, j: (l, 0, n_oc + j)),
                  pl.BlockSpec((None, 2, fd), lambda l, j: (l, 0, 0)),
                  pl.BlockSpec((s, tc), lambda l, j: (0, j % nct))],
        out_specs=[pl.BlockSpec((None, s, tc), lambda l, j: (l, 0, j))] * 2,
        out_shape=[jax.ShapeDtypeStruct((n_layers, s, HY_ORDER * width), F32)] * 2,
        scratch_shapes=[pltpu.VMEM((s, fd), F32)],
        compiler_params=_params(2), name="hy_filter",
    )(z, w1p, b1.reshape(depth, 1, fd), w2, b2.reshape(depth, 1, fd), w3, w3, freq, dec)


def _spectrum_kernel(f_ref, e_ref, o_ref, k_ref, *, n_fft):
    hb = f_ref.shape[0] // 2
    e = e_ref[...].astype(BF16)
    acc_e = _bdot(f_ref[0:hb, :], e)
    acc_o = _bdot(f_ref[hb:, :], o_ref[...])
    nyq = _bdot(f_ref[hb:hb + 16, :], e)[0:1]
    first = lax.broadcasted_iota(jnp.int32, acc_e.shape, 0) + pl.program_id(0) * hb == 0
    scale = jnp.where(first, 1.0 / n_fft, 2.0 / n_fft)
    k_ref[0:hb, :] = acc_e * scale
    k_ref[hb:, :] = jnp.where(first, nyq, acc_o) * scale


def _hy_spectrum(fmat, e, o, *, tmf):
    n_fft, s = fmat.shape
    n_layers, _, n = e.shape
    tn = 512
    return pl.pallas_call(
        functools.partial(_spectrum_kernel, n_fft=n_fft), grid=(n_fft // tmf, n_layers, n // tn),
        in_specs=[pl.BlockSpec((tmf, s), lambda i, l, j: (i, 0)),
                  pl.BlockSpec((None, s, tn), lambda i, l, j: (l, 0, j)),
                  pl.BlockSpec((None, s, tn), lambda i, l, j: (l, 0, j))],
        out_specs=pl.BlockSpec((None, tmf, tn), lambda i, l, j: (l, i, j)),
        out_shape=jax.ShapeDtypeStruct((n_layers, n_fft, n), F32),
        compiler_params=_params(3), name="hy_spectrum",
    )(fmat, e, o)


def _fwd_dft_kernel(f_ref, u_ref, k_ref, y_ref):
    acc = _bdot(f_ref[...], u_ref[...])
    hb = acc.shape[0] // 2
    ua, ub = acc[:hb], acc[hb:]
    ka, kb = k_ref[0:hb, :], k_ref[hb:, :]
    first = lax.broadcasted_iota(jnp.int32, ua.shape, 0) + pl.program_id(0) * hb == 0
    ya = ua * ka - jnp.where(first, jnp.zeros_like(ub), ub * kb)
    yb = jnp.where(first, ub * kb, ua * kb + ub * ka)
    y_ref[0:hb, :] = ya.astype(y_ref.dtype)
    y_ref[hb:, :] = yb.astype(y_ref.dtype)


def _hy_fwd(fmat, u, ucol, kf, layer, order, *, tmf):
    n_fft, s = fmat.shape
    b = u.shape[0]
    c = kf.shape[2] // HY_ORDER
    return pl.pallas_call(
        _fwd_dft_kernel, grid=(n_fft // tmf, b),
        in_specs=[pl.BlockSpec((tmf, s), lambda i, b_: (i, 0)),
                  pl.BlockSpec((None, s, c), lambda i, b_: (b_, 0, ucol)),
                  pl.BlockSpec((None, tmf, c), lambda i, b_: (layer, i, order))],
        out_specs=pl.BlockSpec((None, tmf, c), lambda i, b_: (b_, i, 0)),
        out_shape=jax.ShapeDtypeStruct((b, n_fft, c), BF16),
        compiler_params=_params(2), name="hy_fwd",
    )(fmat, u, kf)


def _inv_dft_kernel(ft_ref, y_ref, u_ref, g_ref, bias_ref, o_ref):
    conv = jnp.dot(ft_ref[...], y_ref[...], preferred_element_type=F32)
    o_ref[...] = (g_ref[...] * (conv + u_ref[...] * bias_ref[...])).astype(o_ref.dtype)


def _hy_inv(fmat_t, yf, u, ucol, gate, gcol, hy_bias, bias_row, out_dtype, *, tmi):
    s, n_fft = fmat_t.shape
    b, _, c = yf.shape
    return pl.pallas_call(
        _inv_dft_kernel, grid=(s // tmi, b),
        in_specs=[pl.BlockSpec((tmi, n_fft), lambda i, b_: (i, 0)),
                  pl.BlockSpec((None, n_fft, c), lambda i, b_: (b_, 0, 0)),
                  pl.BlockSpec((None, tmi, c), lambda i, b_: (b_, i, ucol)),
                  pl.BlockSpec((None, tmi, c), lambda i, b_: (b_, i, gcol)),
                  pl.BlockSpec((None, 1, c), lambda i, b_: (bias_row, 0, 0))],
        out_specs=pl.BlockSpec((None, tmi, c), lambda i, b_: (b_, i, 0)),
        out_shape=jax.ShapeDtypeStruct((b, s, c), out_dtype),
        compiler_params=_params(2), name="hy_inv",
    )(fmat_t, yf, u, gate, hy_bias)


def _pool_kernel(x_ref, w_ref, sc_ref, o_ref):
    x = x_ref[...]
    s = x.shape[0]
    pos = lax.broadcasted_iota(jnp.int32, x.shape, 0)
    grp = pl.program_id(1)
    for gi, win in enumerate(POOL_WINDOWS):
        @pl.when(grp == gi)
        def _(win=win):
            half = win // 2
            tot = x
            for d in range(-half, win - half):
                if d != 0:
                    tot = tot + _shift_rows(x, d, s)
            cnt = jnp.minimum(pos + (win - half), s) - jnp.maximum(pos - half, 0)
            mean = tot / cnt.astype(F32)
            o_ref[...] = (_bdot(mean - x, w_ref[...]) * sc_ref[...]).astype(o_ref.dtype)


def _pool(rest, col_off, pool_w, pool_scale, layer):
    b, s, _ = rest.shape
    depth, groups, gd, _ = pool_w.shape
    off = col_off // gd
    return pl.pallas_call(
        _pool_kernel, grid=(b, groups),
        in_specs=[pl.BlockSpec((None, s, gd), lambda b_, g: (b_, 0, off + g)),
                  pl.BlockSpec((None, None, gd, gd), lambda b_, g: (layer, g, 0, 0)),
                  pl.BlockSpec((None, 1, gd), lambda b_, g: (layer, 0, g))],
        out_specs=pl.BlockSpec((None, s, gd), lambda b_, g: (b_, 0, g)),
        out_shape=jax.ShapeDtypeStruct((b, s, groups * gd), BF16),
        compiler_params=_params(2), name="pool",
    )(rest, pool_w, pool_scale.reshape(depth, 1, groups * gd))


def _merge_kernel(att_ref, hy_ref, po_ref, wa_ref, wh_ref, wp_ref, ga_ref, gh_ref, gp_ref, o_ref):
    acc = _sigmoid(ga_ref[...]) * _bdot(att_ref[...], wa_ref[...])
    acc = acc + _sigmoid(gh_ref[...]) * _bdot(hy_ref[...], wh_ref[...])
    acc = acc + _sigmoid(gp_ref[...]) * _bdot(po_ref[...], wp_ref[...])
    o_ref[...] = acc.astype(o_ref.dtype)


def _merge(att, hyo, poo, rest, gate_off, w_att_o, w_hy_o, w_pool_o, layer, *, tm, tn):
    r = att.shape[0]
    d = w_att_o.shape[2]
    goff = gate_off // tn
    nd = d // tn

    def a_spec(width):
        return pl.BlockSpec((tm, width), lambda i, j: (i, 0))

    def w_spec(width):
        return pl.BlockSpec((None, width, tn), lambda i, j: (layer, 0, j))

    def g_spec(branch):
        return pl.BlockSpec((tm, tn), lambda i, j: (i, goff + branch * nd + j))

    return pl.pallas_call(
        _merge_kernel, grid=(r // tm, nd),
        in_specs=[a_spec(att.shape[1]), a_spec(hyo.shape[1]), a_spec(poo.shape[1]),
                  w_spec(att.shape[1]), w_spec(hyo.shape[1]), w_spec(poo.shape[1]),
                  g_spec(0), g_spec(1), g_spec(2)],
        out_specs=pl.BlockSpec((tm, tn), lambda i, j: (i, j)),
        out_shape=jax.ShapeDtypeStruct((r, d), BF16),
        compiler_params=_params(2), name="merge",
    )(att, hyo, poo, w_att_o, w_hy_o, w_pool_o, rest, rest, rest)


def _gelu_tanh(x):
    return 0.5 * x * (1.0 + jnp.tanh(math.sqrt(2.0 / math.pi) * (x + 0.044715 * (x * x * x))))


def _ffn_up_kernel(a_ref, wg_ref, wv_ref, cw_ref, o_ref, *, seg):
    a = a_ref[...]
    gate = _dwconv3(_bdot(a, wg_ref[...]), cw_ref[...], seg)
    o_ref[...] = (_gelu_tanh(gate) * _bdot(a, wv_ref[...])).astype(o_ref.dtype)


def _ffn_up(h, w_up, conv_w, layer, seg, *, tn):
    g, s, k = h.shape
    ff = w_up.shape[2] // 2
    nt = ff // tn
    return pl.pallas_call(
        functools.partial(_ffn_up_kernel, seg=seg), grid=(g, nt),
        in_specs=[pl.BlockSpec((None, s, k), lambda a_, j: (a_, 0, 0)),
                  pl.BlockSpec((None, k, tn), lambda a_, j: (layer, 0, j)),
                  pl.BlockSpec((None, k, tn), lambda a_, j: (layer, 0, nt + j)),
                  pl.BlockSpec((None, 3, tn), lambda a_, j: (layer, 0, j))],
        out_specs=pl.BlockSpec((None, s, tn), lambda a_, j: (a_, 0, j)),
        out_shape=jax.ShapeDtypeStruct((g, s, ff), BF16),
        compiler_params=_params(2), name="ffn_up",
    )(h, w_up, w_up, conv_w)


def _rope_tables(seq):
    rows = seq // GRID_W
    row = jnp.repeat(jnp.arange(rows, dtype=F32), GRID_W)
    col = jnp.tile(jnp.arange(GRID_W, dtype=F32), rows)
    n_freq = ATT_HEAD_DIM // 4
    inv = ROPE_BASE ** (-jnp.arange(n_freq, dtype=F32) / n_freq)
    ang = jnp.stack([row[:, None] * inv, col[:, None] * inv], axis=1)
    cos, sin = jnp.cos(ang), jnp.sin(ang)
    zero = jnp.zeros_like(sin)
    co = jnp.stack([cos, cos], axis=2).reshape(seq, ATT_HEAD_DIM)
    sa = jnp.stack([-sin, zero], axis=2).reshape(seq, ATT_HEAD_DIM)
    sb = jnp.stack([zero, sin], axis=2).reshape(seq, ATT_HEAD_DIM)
    rep = LANE // ATT_HEAD_DIM
    return tuple(jnp.tile(t, (1, rep)) for t in (co, sa, sb))


def _dft_matrices(s, hb):
    n_fft = 2 * s
    lo = min(s, 64)
    hi = s // lo
    sp = jnp.arange(s, dtype=jnp.int32)[None, :]
    f_lo = jnp.arange(lo, dtype=jnp.int32)[:, None]
    f_hi = jnp.arange(hi, dtype=jnp.int32)[:, None] * lo
    ang_lo = (2.0 * math.pi / n_fft) * ((f_lo * sp) % n_fft).astype(F32)
    ang_hi = (2.0 * math.pi / n_fft) * ((f_hi * sp) % n_fft).astype(F32)
    c1, s1 = jnp.cos(ang_lo)[None], jnp.sin(ang_lo)[None]
    c2, s2 = jnp.cos(ang_hi)[:, None], jnp.sin(ang_hi)[:, None]
    cmat = (c1 * c2 - s1 * s2).reshape(s, s)
    smat = (s1 * c2 + c1 * s2).reshape(s, s)
    nyq = jnp.where(jnp.arange(s) % 2 == 0, 1.0, -1.0).astype(F32)
    smat = smat.at[0].set(nyq)
    fmat = jnp.concatenate([cmat.reshape(s // hb, hb, s), smat.reshape(s // hb, hb, s)], axis=1)
    fmat = fmat.reshape(n_fft, s).astype(BF16)
    return fmat, fmat.T


def _hy_tables(s, width):
    t = jnp.linspace(0.0, 1.0, s, dtype=F32)[:, None]
    w_ang = 2.0 * math.pi * jnp.arange(s, dtype=F32)[:, None] / s
    f = jnp.linspace(1e-4, HY_BANDS - 1, HY_BANDS, dtype=F32)[None, :]
    z = jnp.concatenate([t, jnp.cos(f * w_ang), -jnp.sin(f * w_ang)], axis=-1)
    z = jnp.pad(z, ((0, 0), (0, LANE - HY_EMB_DIM)))
    deltas = jnp.abs(jnp.linspace(HY_MIN_DECAY, HY_MAX_DECAY, width, dtype=F32))
    return z, jnp.exp(-t * deltas[None, :])


def _hyena_setup(s, hb, tmf, tmi, n_layers, p):
    width = p["w_hy_o"].shape[1]
    z, dec = _hy_tables(s, width)
    fmat, fmat_t = _dft_matrices(s, hb)
    e, o = _hy_filters(z, dec, p["hy_ffn_w1"], p["hy_ffn_b1"], p["hy_ffn_w2"], p["hy_ffn_b2"],
                       p["hy_ffn_w3"], p["hy_freq"], n_layers, width)
    return fmat, fmat_t, tmf, tmi, _hy_spectrum(fmat, e, o, tmf=tmf)


def _hyena_branch(conv, layer, p, setup):
    width = p["w_hy_o"].shape[1]
    fmat, fmat_t, tmf, tmi, kf = setup
    depth = p["hy_bias"].shape[0]
    bias = p["hy_bias"].reshape(depth * HY_ORDER, 1, width)
    yf = _hy_fwd(fmat, conv, 0, kf, layer, 0, tmf=tmf)
    zsig = _hy_inv(fmat_t, yf, conv, 0, conv, 1, bias, layer * HY_ORDER, F32, tmi=tmi)
    yf = _hy_fwd(fmat, zsig, 0, kf, layer, 1, tmf=tmf)
    return _hy_inv(fmat_t, yf, zsig, 0, conv, 2, bias, layer * HY_ORDER + 1, BF16, tmi=tmi)


def kernel(x, c, ctx, c_ctx, w_ada, b_ada, norm_g, w_in, diff_lam, attn_subln_g, hy_short_w,
           hy_ffn_w1, hy_ffn_b1, hy_ffn_w2, hy_ffn_b2, hy_ffn_w3, hy_freq, hy_bias, pool_w, pool_scale,
           w_att_o, w_hy_o, w_pool_o, w_out, w_up, ff_conv_w, w_down):
    p = dict(hy_short_w=hy_short_w, hy_ffn_w1=hy_ffn_w1, hy_ffn_b1=hy_ffn_b1, hy_ffn_w2=hy_ffn_w2,
             hy_ffn_b2=hy_ffn_b2, hy_ffn_w3=hy_ffn_w3, hy_freq=hy_freq, hy_bias=hy_bias, w_hy_o=w_hy_o)
    b, seq, d = x.shape
    lc = ctx.shape[1]
    depth = w_ada.shape[0]
    att_w = w_att_o.shape[1]
    hy_w = w_hy_o.shape[1]
    pool_width = w_pool_o.shape[1]
    qkv_w = 3 * att_w
    rest_w = w_in.shape[2] - qkv_w
    gate_off = 3 * hy_w + pool_width

    cond = jnp.concatenate([c, c_ctx[None], jnp.zeros((8 - b - 1, d), F32)], axis=0)
    mod = _ada(cond, w_ada, b_ada)

    rope = _rope_tables(seq)
    tabs_x = _hyena_setup(seq, 512, 1024, 1024, depth, p)
    tabs_c = _hyena_setup(lc, lc, 2 * lc, lc, depth - 1, p)

    ctx = ctx.reshape(1, b * lc, d)

    def mods(layer):
        m = mod[layer].reshape(8, 6, d)
        return [m[:b, k][:, None, :] for k in range(6)], [m[b:b + 1, k][:, None, :] for k in range(6)]

    m_x, m_c = mods(0)
    hx = _resnorm(x, g_norm=norm_g[0, 0], shift=m_x[0], scale=m_x[1])
    hc = _resnorm(ctx, g_norm=norm_g[0, 0], shift=m_c[0], scale=m_c[1])

    for l in range(depth):
        last = l == depth - 1
        lam_init = 0.8 - 0.6 * math.exp(-0.3 * l)
        m_x, m_c = mods(l)
        streams = [("x", x, hx, m_x, seq, b), ("c", ctx, hc, m_c, lc, b)]
        qkv_c = _mm(hc, w_in, l, 0, qkv_w, BF16, tm=b * lc, tn=512, name="mm_qkv").reshape(b, lc, qkv_w)
        qkv_x = _mm(hx, w_in, l, 0, qkv_w, BF16, tm=seq, tn=512, rope=rope, rope_cols=2 * att_w,
                    name="mm_qkv_rope")
        new = {}
        for tag, res, h, m, s_len, nb in streams:
            is_x = tag == "x"
            if last and not is_x:
                continue
            g_rows = res.shape[0] * res.shape[1]
            rest = _mm(h, w_in, l, qkv_w, rest_w, F32, tm=h.shape[1], tn=512, conv=(hy_short_w, s_len),
                       name="mm_rest")
            rest_seq = rest.reshape(nb, s_len, rest_w)
            if is_x:
                att = _attention(qkv_x, qkv_c, diff_lam, attn_subln_g, l, lam_init, tq=1024, n_sub=4)
            else:
                att = _attention(qkv_c, None, diff_lam, attn_subln_g, l, lam_init, tq=s_len, n_sub=2)
            hyo = _hyena_branch(rest_seq, l, p, tabs_x if is_x else tabs_c)
            poo = _pool(rest_seq, 3 * hy_w, pool_w, pool_scale, l)
            merged = _merge(att.reshape(g_rows, att_w), hyo.reshape(g_rows, hy_w),
                            poo.reshape(g_rows, pool_width), rest.reshape(g_rows, rest_w), gate_off,
                            w_att_o, w_hy_o, w_pool_o, l, tm=min(g_rows, 2048), tn=256)
            merged = merged.reshape(h.shape)
            mix = _mm(merged, w_out, l, 0, d, F32, tm=h.shape[1], tn=512, name="mm_out")
            res, h2 = _resnorm(res, mix, norm_g[l, 1], m[2], norm_g[l, 2], m[3], m[4])
            act = _ffn_up(h2, w_up, ff_conv_w, l, s_len, tn=256)
            ffn = _mm(act, w_down, l, 0, d, F32, tm=1024, tn=256, name="mm_down")
            if last:
                res = _resnorm(res, ffn, norm_g[l, 3], m[5])
                h_next = None
            else:
                m_nx, m_nc = mods(l + 1)
                m_n = m_nx if is_x else m_nc
                res, h_next = _resnorm(res, ffn, norm_g[l, 3], m[5], norm_g[l + 1, 0], m_n[0], m_n[1])
            new[tag] = (res, h_next)
        x, hx = new["x"]
        if not last:
            ctx, hc = new["c"]
    return x
```
